```python
import jax, jax.numpy as jnp
from jax import lax
import numpy as np

D_MODEL = 1024
BATCH = 4
SEQ = 4096
DEPTH = 2

CTX_LEN = 256
GRID_W = 64
ROPE_BASE = 10000.0
NORM_EPS = 1e-6

D_RNN = D_MODEL
LRU_BLOCKS = 8
LRU_BLOCK_W = D_RNN // LRU_BLOCKS
CONV_W = 4
CONV_PAD_L = 2
CONV_PAD_R = 1
LRU_C = 8.0

D_RET = D_MODEL
RET_HEADS = 4
RET_HEAD_DIM = D_RET // RET_HEADS
RET_CHUNK = 128

MLA_HEADS = 8
MLA_NOPE = 128
MLA_ROPE = 64
MLA_V = 128
D_MLA = MLA_HEADS * MLA_V
Q_LORA = 384
KV_LORA = 256
MLA_SCALE = (MLA_NOPE + MLA_ROPE) ** -0.5
Q_BLOCK = 128

N_BRANCH = 3
IN_SPLITS = (D_RNN, D_RNN, D_RET, D_RET, D_RET, D_RET, Q_LORA, KV_LORA, MLA_ROPE, D_MLA, N_BRANCH * D_MODEL)
IN_COLS = sum(IN_SPLITS)

kernel_name = 'hybrid_lru_retention_mla_prefix_dit'


def _rms(x, gain):
    xf = x.astype(jnp.float32)
    y = xf * lax.rsqrt(jnp.mean(xf * xf, axis=-1, keepdims=True) + NORM_EPS)
    return (y * gain.astype(jnp.float32)).astype(x.dtype)


def _split_cols(p):
    idx, acc = [], 0
    for s in IN_SPLITS[:-1]:
        acc += s
        idx.append(acc)
    return jnp.split(p, idx, axis=-1)


def _flip(t, rev, axis=1):
    return jnp.flip(t, axis=axis) if rev else t


def _axial_rope_tables(n_tokens, dim):
    rows = n_tokens // GRID_W
    row = jnp.repeat(jnp.arange(rows), GRID_W).astype(jnp.float32)
    col = jnp.tile(jnp.arange(GRID_W), rows).astype(jnp.float32)
    n_freq = dim // 4
    inv = ROPE_BASE ** (-jnp.arange(n_freq, dtype=jnp.float32) / n_freq)
    ang_r = row[:, None] * inv[None, :]
    ang_c = col[:, None] * inv[None, :]
    ang = jnp.concatenate([ang_r, ang_r, ang_c, ang_c], axis=-1)
    return jnp.cos(ang), jnp.sin(ang)


def _rope(x, cos, sin):
    q = x.shape[-1] // 4
    x1, x2, x3, x4 = x[..., :q], x[..., q:2 * q], x[..., 2 * q:3 * q], x[..., 3 * q:]
    rot = jnp.concatenate([-x2, x1, -x4, x3], axis=-1)
    return (x * cos + rot * sin).astype(x.dtype)


def _dwconv(u, w, b):
    T = u.shape[1]
    up = jnp.pad(u, ((0, 0), (CONV_PAD_L, CONV_PAD_R), (0, 0)))
    out = up[:, 0:T] * w[0]
    for j in range(1, CONV_W):
        out = out + up[:, j:j + T] * w[j]
    return out + b


def _lru_coeffs(u, wa, ba, wx, bx, lam):
    B, T, _ = u.shape
    ub = u.reshape(B, T, LRU_BLOCKS, LRU_BLOCK_W)
    r = jax.nn.sigmoid(jnp.einsum('btnc,ncd->btnd', ub, wa).reshape(B, T, D_RNN) + ba)
    i = jax.nn.sigmoid(jnp.einsum('btnc,ncd->btnd', ub, wx).reshape(B, T, D_RNN) + bx)
    log_a = -LRU_C * r * jax.nn.softplus(-lam.astype(jnp.float32))
    a = jnp.exp(log_a)
    b = jnp.sqrt(-jnp.expm1(2.0 * log_a)) * (i * u)
    return a, b


def _linear_scan(a, b, h0):
    def comb(l, r):
        return (l[0] * r[0], r[0] * l[1] + r[1])
    a_cum, h = lax.associative_scan(comb, (a, b), axis=1)
    return h + a_cum * h0[:, None, :]


def _rglru_branch(xa, xa_c, conv_w, conv_b, wa, ba, wx, bx, lam, with_ctx):
    u = _dwconv(xa.astype(jnp.float32), conv_w, conv_b)
    uc = _dwconv(xa_c.astype(jnp.float32), conv_w, conv_b)
    h_sum = jnp.zeros_like(u)
    hc_parts = []
    for d in range(2):
        rev = d == 1
        a, b = _lru_coeffs(u, wa[d], ba[d], wx[d], bx[d], lam[d])
        ac, bc = _lru_coeffs(uc, wa[d], ba[d], wx[d], bx[d], lam[d])
        hc = _linear_scan(_flip(ac, rev), _flip(bc, rev), jnp.zeros_like(uc[:, 0]))
        h = _linear_scan(_flip(a, rev), _flip(b, rev), hc[:, -1])
        h_sum = h_sum + _flip(h, rev)
        if with_ctx:
            hc_parts.append(_flip(hc, rev))
    y = h_sum.astype(xa.dtype)
    yc = (hc_parts[0] + hc_parts[1]).astype(xa.dtype) if with_ctx else None
    return y, yc


def _retention_chunks(q, k, v, log_g, s0):
    B, H, T, _ = q.shape
    dv = v.shape[-1]
    n = T // RET_CHUNK
    pos = jnp.arange(RET_CHUNK, dtype=jnp.float32)
    diff = pos[:, None] - pos[None, :]
    inner = jnp.where(diff >= 0, jnp.exp(log_g[:, None, None] * jnp.maximum(diff, 0.0)), 0.0)
    q_dec = jnp.exp(log_g[:, None] * (pos + 1.0))[None, :, :, None]
    k_dec = jnp.exp(log_g[:, None] * (RET_CHUNK - 1.0 - pos))[None, :, :, None]
    c_dec = jnp.exp(log_g * RET_CHUNK)[None, :, None, None]

    def chunks(t):
        return jnp.moveaxis(t.reshape(B, H, n, RET_CHUNK, t.shape[-1]), 2, 0)

    def step(s, blk):
        qc, kc, vc = blk
        att = jnp.einsum('bhnd,bhmd->bhnm', qc, kc) * inner
        o = jnp.einsum('bhnm,bhme->bhne', att, vc) + jnp.einsum('bhnd,bhde->bhne', qc, s) * q_dec
        s = s * c_dec + jnp.einsum('bhmd,bhme->bhde', kc * k_dec, vc)
        return s, o

    s, o = lax.scan(step, s0, (chunks(q), chunks(k), chunks(v)))
    return jnp.moveaxis(o, 0, 2).reshape(B, H, T, dv), s


def _head_layernorm(o, gain):
    B, H, T, dv = o.shape
    oc = o - jnp.mean(o, axis=-1, keepdims=True)
    y = oc * lax.rsqrt(jnp.mean(oc * oc, axis=-1, keepdims=True) + NORM_EPS)
    y = y * gain.astype(jnp.float32).reshape(H, 1, dv)
    return y.transpose(0, 2, 1, 3).reshape(B, T, H * dv)


def _retention_branch(q, k, v, qc, kc, vc, theta, gain, rope, with_ctx):
    cos, sin = rope
    B = q.shape[0]
    k_scale = RET_HEAD_DIM ** -0.5

    def heads(t):
        return t.astype(jnp.float32).reshape(t.shape[0], t.shape[1], RET_HEADS, RET_HEAD_DIM)

    def bhtd(t):
        return t.transpose(0, 2, 1, 3)

    q_l = bhtd(_rope(heads(q), cos[:, None], sin[:, None]))
    k_l = bhtd(_rope(heads(k), cos[:, None], sin[:, None]) * k_scale)
    v_l = bhtd(heads(v))
    q_c, k_c, v_c = bhtd(heads(qc)), bhtd(heads(kc) * k_scale), bhtd(heads(vc))
    s0 = jnp.zeros((B, RET_HEADS, RET_HEAD_DIM, RET_HEAD_DIM), jnp.float32)
    o_sum = jnp.zeros_like(v_l)
    oc_parts = []
    for d in range(2):
        rev = d == 1
        log_g = jax.nn.log_sigmoid(theta[d].astype(jnp.float32))
        oc, s_ctx = _retention_chunks(_flip(q_c, rev, 2), _flip(k_c, rev, 2), _flip(v_c, rev, 2), log_g, s0)
        o, _ = _retention_chunks(_flip(q_l, rev, 2), _flip(k_l, rev, 2), _flip(v_l, rev, 2), log_g, s_ctx)
        o_sum = o_sum + _flip(o, rev, 2)
        if with_ctx:
            oc_parts.append(_flip(oc, rev, 2))
    y = _head_layernorm(o_sum, gain).astype(q.dtype)
    yc = _head_layernorm(oc_parts[0] + oc_parts[1], gain).astype(q.dtype) if with_ctx else None
    return y, yc


def _mla_qkv(qd, kvd, kr, q_norm, w_q_up, kv_norm, w_kv_up, g_qn, g_qr, g_kn, g_kr, rope):
    B, T, _ = qd.shape
    q = (_rms(qd, q_norm) @ w_q_up).reshape(B, T, MLA_HEADS, MLA_NOPE + MLA_ROPE)
    kv = (_rms(kvd, kv_norm) @ w_kv_up).reshape(B, T, MLA_HEADS, MLA_NOPE + MLA_V)
    q_nope = _rms(q[..., :MLA_NOPE], g_qn)
    q_rope = _rms(q[..., MLA_NOPE:], g_qr)
    k_nope = _rms(kv[..., :MLA_NOPE], g_kn)
    v = kv[..., MLA_NOPE:]
    k_rope = _rms(kr, g_kr)
    if rope is not None:
        cos, sin = rope
        q_rope = _rope(q_rope, cos[:, None], sin[:, None])
        k_rope = _rope(k_rope, cos, sin)
    return q_nope, q_rope, k_nope, k_rope, v


def _mla_attend(q_nope, q_rope, k_nope, k_rope, v):
    s = (jnp.einsum('bqhd,bkhd->bhqk', q_nope, k_nope)
         + jnp.einsum('bqhr,bkr->bhqk', q_rope, k_rope)).astype(jnp.float32) * MLA_SCALE
    p = jax.nn.softmax(s, axis=-1)
    return jnp.einsum('bhqk,bkhe->bqhe', p.astype(v.dtype), v)


def _mla_latent(q_nope, q_rope, k_nope, k_rope, v):
    B, T = q_nope.shape[0], q_nope.shape[1]
    nb = T // Q_BLOCK

    def blocks(t):
        return jnp.moveaxis(t.reshape((B, nb, Q_BLOCK) + t.shape[2:]), 1, 0)

    out = lax.map(lambda qs: _mla_attend(qs[0], qs[1], k_nope, k_rope, v), (blocks(q_nope), blocks(q_rope)))
    return jnp.moveaxis(out, 0, 1).reshape(B, T, D_MLA)


def _merge(lp, ya, yb, yc, ga, gb, gc, gm):
    m_a, m_b, m_c = jnp.split(jax.nn.sigmoid(gm), 3, axis=-1)
    z = (m_a * ((ya * jax.nn.silu(ga)) @ lp['w_down_a'])
         + m_b * ((yb * jax.nn.silu(gb)) @ lp['w_down_b'])
         + m_c * ((yc * jax.nn.silu(gc)) @ lp['w_down_c']))
    return z @ lp['w_out']


def _layer(x, xc, c, c_ctx, lp, rope_ret, rope_mla, with_ctx):
    B, L = xc.shape[0], xc.shape[1]
    mod = jax.nn.silu(c) @ lp['w_mod'] + lp['b_mod']
    mod_c = jax.nn.silu(c_ctx) @ lp['w_mod'] + lp['b_mod']
    shift, scale, gate = jnp.split(mod, 3, axis=-1)
    shift_c, scale_c, gate_c = jnp.split(mod_c, 3, axis=-1)
    h = _rms(x, lp['norm_gain']) * (1.0 + scale[:, None]) + shift[:, None]
    hc = _rms(xc, lp['norm_gain']) * (1.0 + scale_c) + shift_c
    (xa, ga, qb, kb, vb, gb, qd, kvd, kr, gc, gm) = _split_cols(h @ lp['w_in'])
    (xa_c, ga_c, qb_c, kb_c, vb_c, gb_c, qd_c, kvd_c, kr_c, gc_c, gm_c) = _split_cols(hc @ lp['w_in'])

    ya, ya_c = _rglru_branch(xa, xa_c, lp['conv_w'], lp['conv_b'], lp['lru_wa'], lp['lru_ba'],
                             lp['lru_wx'], lp['lru_bx'], lp['lru_lambda'], with_ctx)
    yb, yb_c = _retention_branch(qb, kb, vb, qb_c, kb_c, vb_c, lp['ret_theta'], lp['ret_gain'], rope_ret, with_ctx)

    mla_w = (lp['mla_q_norm'], lp['mla_w_q_up'], lp['mla_kv_norm'], lp['mla_w_kv_up'],
             lp['mla_g_qn'], lp['mla_g_qr'], lp['mla_g_kn'], lp['mla_g_kr'])
    qn, qr, kn, krot, v = _mla_qkv(qd, kvd, kr, *mla_w, rope_mla)
    qn_c, qr_c, kn_c, krot_c, v_c = _mla_qkv(qd_c, kvd_c, kr_c, *mla_w, None)
    yc = _mla_latent(qn, qr, jnp.concatenate([kn_c, kn], axis=1),
                     jnp.concatenate([krot_c, krot], axis=1), jnp.concatenate([v_c, v], axis=1))

    x = x + (gate[:, None] * _merge(lp, ya, yb, yc, ga, gb, gc, gm)).astype(x.dtype)
    if with_ctx:
        yc_c = _mla_attend(qn_c, qr_c, kn_c, krot_c, v_c).reshape(B, L, D_MLA)
        xc = xc + (gate_c * _merge(lp, ya_c, yb_c, yc_c, ga_c, gb_c, gc_c, gm_c)).astype(xc.dtype)
    return x, xc


def setup_inputs(seed: int = 0) -> dict:
    key = jax.random.key(seed)
    ks = jax.random.split(key, 32)
    f32 = jnp.float32

    def nrm(k, shape, fan):
        return jax.random.normal(k, shape, f32) * (fan ** -0.5)

    def small(k, shape):
        return jax.random.normal(k, shape, f32) * 0.02

    def gain(k, shape):
        return 1.0 + 0.02 * jax.random.normal(k, shape, f32)

    u = jax.random.uniform(ks[10], (DEPTH, 2, D_RNN), f32, minval=0.9, maxval=0.999)
    a0 = u ** (1.0 / LRU_C)
    lru_lambda = jnp.log(a0) - jnp.log1p(-a0)
    ret_base = jnp.log(2.0 ** (5.0 + jnp.arange(RET_HEADS, dtype=f32)) - 1.0)
    ret_theta = ret_base + 0.05 * jax.random.normal(ks[11], (DEPTH, 2, RET_HEADS), f32)

    return {
        'x': jax.random.normal(ks[0], (BATCH, SEQ, D_MODEL), f32),
        'c': jax.random.normal(ks[1], (BATCH, D_MODEL), f32),
        'ctx': jax.random.normal(ks[2], (BATCH, CTX_LEN, D_MODEL), f32),
        'c_ctx': jax.random.normal(ks[3], (D_MODEL,), f32),
        'w_mod': nrm(ks[4], (DEPTH, D_MODEL, 3 * D_MODEL), D_MODEL),
        'b_mod': small(ks[5], (DEPTH, 3 * D_MODEL)),
        'norm_gain': gain(ks[6], (DEPTH, D_MODEL)),
        'w_in': nrm(ks[7], (DEPTH, D_MODEL, IN_COLS), D_MODEL),
        'conv_w': nrm(ks[8], (DEPTH, CONV_W, D_RNN), CONV_W),
        'conv_b': small(ks[9], (DEPTH, D_RNN)),
        'lru_wa': nrm(ks[12], (DEPTH, 2, LRU_BLOCKS, LRU_BLOCK_W, LRU_BLOCK_W), LRU_BLOCK_W),
        'lru_ba': small(ks[13], (DEPTH, 2, D_RNN)),
        'lru_wx': nrm(ks[14], (DEPTH, 2, LRU_BLOCKS, LRU_BLOCK_W, LRU_BLOCK_W), LRU_BLOCK_W),
        'lru_bx': small(ks[15], (DEPTH, 2, D_RNN)),
        'lru_lambda': lru_lambda,
        'ret_theta': ret_theta,
        'ret_gain': gain(ks[16], (DEPTH, D_RET)),
        'mla_q_norm': gain(ks[17], (DEPTH, Q_LORA)),
        'mla_w_q_up': nrm(ks[18], (DEPTH, Q_LORA, MLA_HEADS * (MLA_NOPE + MLA_ROPE)), Q_LORA),
        'mla_kv_norm': gain(ks[19], (DEPTH, KV_LORA)),
        'mla_w_kv_up': nrm(ks[20], (DEPTH, KV_LORA, MLA_HEADS * (MLA_NOPE + MLA_V)), KV_LORA),
        'mla_g_qn': gain(ks[21], (DEPTH, MLA_NOPE)),
        'mla_g_qr': gain(ks[22], (DEPTH, MLA_ROPE)),
        'mla_g_kn': gain(ks[23], (DEPTH, MLA_NOPE)),
        'mla_g_kr': gain(ks[24], (DEPTH, MLA_ROPE)),
        'w_down_a': nrm(ks[25], (DEPTH, D_RNN, D_MODEL), D_RNN),
        'w_down_b': nrm(ks[26], (DEPTH, D_RET, D_MODEL), D_RET),
        'w_down_c': nrm(ks[27], (DEPTH, D_MLA, D_MODEL), D_MLA),
        'w_out': nrm(ks[28], (DEPTH, D_MODEL, D_MODEL), D_MODEL),
    }


def reference(x, c, ctx, c_ctx, w_mod, b_mod, norm_gain, w_in, conv_w, conv_b, lru_wa, lru_ba, lru_wx, lru_bx,
              lru_lambda, ret_theta, ret_gain, mla_q_norm, mla_w_q_up, mla_kv_norm, mla_w_kv_up,
              mla_g_qn, mla_g_qr, mla_g_kn, mla_g_kr, w_down_a, w_down_b, w_down_c, w_out):
    n_lat = x.shape[1]
    rope_ret = _axial_rope_tables(n_lat, RET_HEAD_DIM)
    rope_mla = _axial_rope_tables(n_lat, MLA_ROPE)
    xc = ctx
    for l in range(DEPTH):
        lp = {
            'w_mod': w_mod[l], 'b_mod': b_mod[l], 'norm_gain': norm_gain[l], 'w_in': w_in[l],
            'conv_w': conv_w[l], 'conv_b': conv_b[l], 'lru_wa': lru_wa[l], 'lru_ba': lru_ba[l],
            'lru_wx': lru_wx[l], 'lru_bx': lru_bx[l], 'lru_lambda': lru_lambda[l],
            'ret_theta': ret_theta[l], 'ret_gain': ret_gain[l],
            'mla_q_norm': mla_q_norm[l], 'mla_w_q_up': mla_w_q_up[l], 'mla_kv_norm': mla_kv_norm[l],
            'mla_w_kv_up': mla_w_kv_up[l], 'mla_g_qn': mla_g_qn[l], 'mla_g_qr': mla_g_qr[l],
            'mla_g_kn': mla_g_kn[l], 'mla_g_kr': mla_g_kr[l],
            'w_down_a': w_down_a[l], 'w_down_b': w_down_b[l], 'w_down_c': w_down_c[l], 'w_out': w_out[l],
        }
        x, xc = _layer(x, xc, c, c_ctx, lp, rope_ret, rope_mla, l < DEPTH - 1)
    return x
```

```python
import functools
import math

import jax
import jax.numpy as jnp
from jax import lax
from jax.experimental import pallas as pl
from jax.experimental.pallas import tpu as pltpu

D_MODEL = 1024
GRID_W = 64
ROPE_BASE = 10000.0
NORM_EPS = 1e-6

LRU_BLOCKS = 8
LRU_BLOCK_W = D_MODEL // LRU_BLOCKS
CONV_W = 4
LRU_C = 8.0

RET_HEADS = 4
RET_HEAD_DIM = D_MODEL // RET_HEADS
RET_CHUNK = 128

MLA_HEADS = 8
MLA_NOPE = 128
MLA_ROPE = 64
MLA_V = 128
Q_LORA = 384
KV_LORA = 256
MLA_SCALE = (MLA_NOPE + MLA_ROPE) ** -0.5
MLA_QK_PAD = 256

P_COLS = 11 * 1024
COL_XA, COL_GA, COL_QB, COL_KB, COL_VB, COL_GB, COL_GC, COL_GM, COL_TAIL = (
    0, 1024, 2048, 3072, 4096, 5120, 6144, 7168, 10240)

MXU_DT = jnp.bfloat16
SUBTILE = 256
VMEM_LIMIT = 56 * 1024 * 1024
F32 = jnp.float32


def _params(sem):
    return pltpu.CompilerParams(dimension_semantics=sem, vmem_limit_bytes=VMEM_LIMIT)


def _silu(x):
    return x * jax.nn.sigmoid(x)


def _softplus(x):
    return jnp.maximum(x, 0.0) + jnp.log1p(jnp.exp(-jnp.abs(x)))


def _pick(n, cands):
    for c in cands:
        if n % c == 0:
            return c
    raise ValueError(f"no tile in {cands} divides {n}")


def _mod_kernel(c_ref, w_ref, b_ref, o_ref):
    a = _silu(c_ref[...])
    o_ref[...] = jnp.dot(a, w_ref[...], preferred_element_type=F32) + b_ref[...]


def _modulation(cc, w_mod, b_mod):
    rows = cc.shape[0]
    n3 = w_mod.shape[1]
    tn = 1024
    return pl.pallas_call(
        _mod_kernel,
        grid=(n3 // tn,),
        in_specs=[pl.BlockSpec((rows, D_MODEL), lambda j: (0, 0)),
                  pl.BlockSpec((D_MODEL, tn), lambda j: (0, j)),
                  pl.BlockSpec((1, tn), lambda j: (0, j))],
        out_specs=pl.BlockSpec((rows, tn), lambda j: (0, j)),
        out_shape=jax.ShapeDtypeStruct((rows, n3), F32),
        compiler_params=_params(("arbitrary",)),
        name="modulation",
    )(cc, w_mod, b_mod.reshape(1, n3))


def _inproj_kernel(x_ref, mod_ref, gain_ref, w_ref, o_ref, h_ref, *, tm):
    @pl.when(pl.program_id(1) == 0)
    def _():
        gain = gain_ref[...]
        for s in range(tm // SUBTILE):
            rows = slice(s * SUBTILE, (s + 1) * SUBTILE)
            x = x_ref[rows, :]
            y = x * lax.rsqrt(jnp.mean(x * x, axis=-1, keepdims=True) + NORM_EPS) * gain
            h_ref[rows, :] = (y * (1.0 + mod_ref[s, 1:2, :]) + mod_ref[s, 0:1, :]).astype(h_ref.dtype)

    o_ref[...] = jnp.dot(h_ref[...], w_ref[...], preferred_element_type=F32)


def _inproj(xs2, modrows, gain, w_in_p):
    n = xs2.shape[0]
    tm = _pick(n, (1024, 512, 256))
    tn = 1024
    return pl.pallas_call(
        functools.partial(_inproj_kernel, tm=tm),
        grid=(n // tm, P_COLS // tn),
        in_specs=[pl.BlockSpec((tm, D_MODEL), lambda i, j: (i, 0)),
                  pl.BlockSpec((tm // SUBTILE, 8, D_MODEL), lambda i, j: (i, 0, 0)),
                  pl.BlockSpec((1, D_MODEL), lambda i, j: (0, 0)),
                  pl.BlockSpec((D_MODEL, tn), lambda i, j: (0, j))],
        out_specs=pl.BlockSpec((tm, tn), lambda i, j: (i, j)),
        out_shape=jax.ShapeDtypeStruct((n, P_COLS), F32),
        scratch_shapes=[pltpu.VMEM((tm, D_MODEL), MXU_DT)],
        compiler_params=_params(("arbitrary", "arbitrary")),
        name="inproj",
    )(xs2, modrows, gain.reshape(1, D_MODEL), w_in_p)


LRU_CHUNK = 128
SCAN_GROUP = 8


def _lru_kernel(xa_ref, ga_ref, cw_ref, cb_ref, w_ref, ba_ref, bx_ref, lam_ref,
                o_ref, u_s, hf_s, xe_s, as_s, bs_s, hin_s, *, n_ctx, n_lat):
    C = LRU_CHUNK
    W = LRU_BLOCK_W
    G = C // SCAN_GROUP
    S = (n_ctx + n_lat) * C
    cw = cw_ref[...]
    cb = cb_ref[...]
    rmod = lax.broadcasted_iota(jnp.int32, (C, W), 0) % SCAN_GROUP

    def conv_chunk(start, first, last):
        x = xa_ref[0, pl.ds(start, C), :]
        prev = xa_ref[0, pl.ds(jnp.maximum(start - 8, 0), 8), :]
        nxt = xa_ref[0, pl.ds(jnp.minimum(start + C, S - 8), 8), :]
        xe_s[0:8, :] = jnp.where(first, 0.0, prev)
        xe_s[8:8 + C, :] = x
        xe_s[8 + C:, :] = jnp.where(last, 0.0, nxt)
        return (xe_s[6:6 + C, :] * cw[0:1] + xe_s[7:7 + C, :] * cw[1:2]
                + x * cw[2:3] + xe_s[9:9 + C, :] * cw[3:4] + cb)

    def coeffs(u, d):
        z = jnp.dot(u.astype(MXU_DT), w_ref[d, 0], preferred_element_type=F32)
        r = jax.nn.sigmoid(z[:, :W] + ba_ref[d:d + 1, :])
        i = jax.nn.sigmoid(z[:, W:] + bx_ref[d:d + 1, :])
        log_a = (-LRU_C * r) * _softplus(-lam_ref[d:d + 1, :])
        a = jnp.exp(log_a)
        t = jnp.tanh(log_a)
        b = jnp.sqrt(-2.0 * t / (1.0 - t)) * (i * u)
        return a, b

    def scan_chunk(a, b, h, reverse):
        s = 1
        while s < SCAN_GROUP:
            if reverse:
                a_o = pltpu.roll(a, C - s, 0)
                b_o = pltpu.roll(b, C - s, 0)
                m = rmod < SCAN_GROUP - s
            else:
                a_o = pltpu.roll(a, s, 0)
                b_o = pltpu.roll(b, s, 0)
                m = rmod >= s
            b = jnp.where(m, a * b_o + b, b)
            a = jnp.where(m, a * a_o, a)
            s *= 2
        as_s[...] = a
        bs_s[...] = b
        order = range(G - 1, -1, -1) if reverse else range(G)
        edge = 0 if reverse else SCAN_GROUP - 1
        for k in order:
            r0 = k * SCAN_GROUP
            hin_s[r0:r0 + SCAN_GROUP, :] = jnp.broadcast_to(h, (SCAN_GROUP, W))
            h = bs_s[r0 + edge:r0 + edge + 1, :] + as_s[r0 + edge:r0 + edge + 1, :] * h
        return b + a * hin_s[...], h

    zero_h = jnp.zeros((1, W), F32)

    def fwd_body(c, h):
        start = pl.multiple_of(c * C, C)
        first = jnp.logical_or(c == 0, c == n_ctx)
        last = jnp.logical_or(c == n_ctx - 1, c == n_ctx + n_lat - 1)
        u = conv_chunk(start, first, last)
        u_s[pl.ds(start, C), :] = u
        a, b = coeffs(u, 0)
        hfull, h = scan_chunk(a, b, h, False)
        hf_s[pl.ds(start, C), :] = hfull
        return h

    lax.fori_loop(0, n_ctx + n_lat, fwd_body, zero_h)

    def bwd_body(c, h):
        start = pl.multiple_of(c * C, C)
        u = u_s[pl.ds(start, C), :]
        a, b = coeffs(u, 1)
        hfull, h = scan_chunk(a, b, h, True)
        y = hf_s[pl.ds(start, C), :] + hfull
        o_ref[0, pl.ds(start, C), :] = (y * _silu(ga_ref[0, pl.ds(start, C), :])).astype(o_ref.dtype)
        return h

    h = lax.fori_loop(0, n_ctx, lambda i, h: bwd_body(n_ctx - 1 - i, h), zero_h)
    lax.fori_loop(0, n_lat, lambda i, h: bwd_body(n_ctx + n_lat - 1 - i, h), h)


def _lru_branch(p3, conv_w, conv_b, w_cat, ba, bx, lam, n_ctx_rows):
    B, S, _ = p3.shape
    C, W = LRU_CHUNK, LRU_BLOCK_W
    n_ctx, n_lat = n_ctx_rows // C, (S - n_ctx_rows) // C
    xa0, ga0 = COL_XA // W, COL_GA // W
    return pl.pallas_call(
        functools.partial(_lru_kernel, n_ctx=n_ctx, n_lat=n_lat),
        grid=(B, LRU_BLOCKS),
        in_specs=[pl.BlockSpec((1, S, W), lambda b, n: (b, 0, xa0 + n)),
                  pl.BlockSpec((1, S, W), lambda b, n: (b, 0, ga0 + n)),
                  pl.BlockSpec((CONV_W, W), lambda b, n: (0, n)),
                  pl.BlockSpec((1, W), lambda b, n: (0, n)),
                  pl.BlockSpec((2, 1, W, 2 * W), lambda b, n: (0, n, 0, 0)),
                  pl.BlockSpec((2, W), lambda b, n: (0, n)),
                  pl.BlockSpec((2, W), lambda b, n: (0, n)),
                  pl.BlockSpec((2, W), lambda b, n: (0, n))],
        out_specs=pl.BlockSpec((1, S, W), lambda b, n: (b, 0, n)),
        out_shape=jax.ShapeDtypeStruct((B, S, D_MODEL), MXU_DT),
        scratch_shapes=[pltpu.VMEM((S, W), F32), pltpu.VMEM((S, W), F32),
                        pltpu.VMEM((C + 16, W), F32), pltpu.VMEM((C, W), F32),
                        pltpu.VMEM((C, W), F32), pltpu.VMEM((C, W), F32)],
        compiler_params=_params(("arbitrary", "arbitrary")),
        name="lru",
    )(p3, p3, conv_w, conv_b.reshape(1, D_MODEL), w_cat, ba, bx, lam)


def _swap_quarters(x, q):
    lane = lax.broadcasted_iota(jnp.int32, (x.shape[0], 128), 1)
    cols = []
    for c in range(x.shape[1] // 128):
        xc = x[:, c * 128:(c + 1) * 128]
        cols.append(jnp.where(lane % (2 * q) < q, pltpu.roll(xc, 128 - q, 1), pltpu.roll(xc, q, 1)))
    return cols[0] if len(cols) == 1 else jnp.concatenate(cols, axis=1)


def _ret_chunk_index(d, c, n_ctx, n_all):
    rev = jnp.where(c < n_ctx, n_ctx - 1 - c, n_ctx + n_all - 1 - c)
    return jnp.where(d == 0, c, rev)


def _ret_kernel(q_ref, k_ref, v_ref, g_ref, cos_ref, sin_ref, theta_ref, gain_ref,
                o_ref, s_s, of_s, *, n_ctx, n_all):
    C, dh = RET_CHUNK, RET_HEAD_DIM
    d = pl.program_id(1)
    c = pl.program_id(2)
    ci = _ret_chunk_index(d, c, n_ctx, n_all)
    start = pl.multiple_of(ci * C, C)

    @pl.when(c == 0)
    def _():
        s_s[...] = jnp.zeros_like(s_s)

    cos = cos_ref[...]
    sin = sin_ref[...]
    fwd = d == 0
    row = lax.broadcasted_iota(jnp.int32, (C, C), 0)
    col = lax.broadcasted_iota(jnp.int32, (C, C), 1)
    diff = jnp.where(fwd, row - col, col - row).astype(F32)
    q_pow = jnp.where(fwd, row + 1, C - row).astype(F32)
    k_pow = jnp.where(fwd, C - 1 - row, row).astype(F32)
    k_scale = dh ** -0.5
    wide = lambda t: jnp.concatenate([t] * (dh // C), axis=1)

    for h in range(RET_HEADS):
        cs = slice(h * dh, (h + 1) * dh)
        lg = -_softplus(-theta_ref[pl.ds(d * RET_HEADS + h, 1), :])
        inner = jnp.where(diff >= 0, jnp.exp(lg * jnp.maximum(diff, 0.0)), 0.0)
        q_dec = wide(jnp.exp(lg * q_pow))
        k_dec = wide(jnp.exp(lg * k_pow))
        c_dec = wide(jnp.exp(lg * float(C)))
        q = q_ref[0, :, cs]
        k = k_ref[0, :, cs]
        q = q * cos + _swap_quarters(q, dh // 4) * sin
        k = (k * cos + _swap_quarters(k, dh // 4) * sin) * k_scale
        qm = q.astype(MXU_DT)
        km = k.astype(MXU_DT)
        vm = v_ref[0, :, cs].astype(MXU_DT)
        s_old = s_s[h]
        att = lax.dot_general(qm, km, (((1,), (1,)), ((), ())), preferred_element_type=F32) * inner
        o = (jnp.dot(att.astype(MXU_DT), vm, preferred_element_type=F32)
             + jnp.dot(qm, s_old.astype(MXU_DT), preferred_element_type=F32) * q_dec)
        kt = jnp.transpose(k * k_dec).astype(MXU_DT)
        s_s[h] = s_old * c_dec + jnp.dot(kt, vm, preferred_element_type=F32)

        @pl.when(fwd)
        def _():
            of_s[pl.ds(start, C), cs] = o

        @pl.when(jnp.logical_not(fwd))
        def _():
            t = o + of_s[pl.ds(start, C), cs]
            tc = t - jnp.mean(t, axis=-1, keepdims=True)
            y = tc * lax.rsqrt(jnp.mean(tc * tc, axis=-1, keepdims=True) + NORM_EPS) * gain_ref[:, cs]
            o_ref[0, :, cs] = (y * _silu(g_ref[0, :, cs])).astype(o_ref.dtype)


def _ret_branch(p3, cos_t, sin_t, theta, gain, n_ctx_rows):
    B, S, _ = p3.shape
    C = RET_CHUNK
    n_ctx, n_all = n_ctx_rows // C, S // C
    cidx = functools.partial(_ret_chunk_index, n_ctx=n_ctx, n_all=n_all)

    def pcol(col):
        return pl.BlockSpec((1, C, D_MODEL), lambda b, d, c: (b, cidx(d, c), col // D_MODEL))

    def out_map(b, d, c):
        return (b, jnp.where(d == 0, cidx(1, 0), cidx(d, c)), 0)

    return pl.pallas_call(
        functools.partial(_ret_kernel, n_ctx=n_ctx, n_all=n_all),
        grid=(B, 2, n_all),
        in_specs=[pcol(COL_QB), pcol(COL_KB), pcol(COL_VB), pcol(COL_GB),
                  pl.BlockSpec((C, RET_HEAD_DIM), lambda b, d, c: (cidx(d, c), 0)),
                  pl.BlockSpec((C, RET_HEAD_DIM), lambda b, d, c: (cidx(d, c), 0)),
                  pl.BlockSpec((2 * RET_HEADS, C), lambda b, d, c: (0, 0)),
                  pl.BlockSpec((1, D_MODEL), lambda b, d, c: (0, 0))],
        out_specs=pl.BlockSpec((1, C, D_MODEL), out_map),
        out_shape=jax.ShapeDtypeStruct((B, S, D_MODEL), MXU_DT),
        scratch_shapes=[pltpu.VMEM((RET_HEADS, RET_HEAD_DIM, RET_HEAD_DIM), F32),
                        pltpu.VMEM((S, D_MODEL), F32)],
        compiler_params=_params(("arbitrary", "arbitrary", "arbitrary")),
        name="retention",
    )(p3, p3, p3, p3, cos_t, sin_t, jnp.broadcast_to(theta.reshape(2 * RET_HEADS, 1), (2 * RET_HEADS, C)),
      gain.reshape(1, D_MODEL))


def _rms_rows(x, gain, width):
    return x * lax.rsqrt(jnp.sum(x * x, axis=-1, keepdims=True) * (1.0 / width) + NORM_EPS) * gain


def _mla_prep_kernel(p_ref, wq_ref, wkv_ref, qn_ref, kvn_ref, gqn_ref, gqr_ref, gkn_ref, gkr_ref,
                     cos_ref, sin_ref, q_ref, k_ref, v_ref):
    H = MLA_HEADS
    blk = p_ref[0]
    qd = blk[:, 0:Q_LORA]
    kvd = blk[:, Q_LORA:Q_LORA + KV_LORA]
    kr = blk[:, Q_LORA + KV_LORA:Q_LORA + KV_LORA + 128]
    tm = blk.shape[0]
    lane = lax.broadcasted_iota(jnp.int32, (tm, 128), 1)
    low = lane < MLA_ROPE
    cos = cos_ref[...]
    sin = sin_ref[...]

    q = jnp.dot(_rms_rows(qd, qn_ref[...], Q_LORA).astype(MXU_DT), wq_ref[...], preferred_element_type=F32)
    kv = jnp.dot(_rms_rows(kvd, kvn_ref[...], KV_LORA).astype(MXU_DT), wkv_ref[...], preferred_element_type=F32)

    k_rope = _rms_rows(kr, gkr_ref[...], MLA_ROPE)
    k_rope = k_rope * cos[:, 0:128] + _swap_quarters(k_rope, MLA_ROPE // 4) * sin[:, 0:128]
    k_rope = jnp.where(low, k_rope, 0.0).astype(k_ref.dtype)

    for j in range(H // 2):
        xr = q[:, H * MLA_NOPE + j * 128:H * MLA_NOPE + (j + 1) * 128]
        sq = xr * xr
        ms_lo = jnp.sum(jnp.where(low, sq, 0.0), axis=-1, keepdims=True)
        ms_hi = jnp.sum(jnp.where(low, 0.0, sq), axis=-1, keepdims=True)
        inv = lax.rsqrt(jnp.where(low, ms_lo, ms_hi) * (1.0 / MLA_ROPE) + NORM_EPS)
        xr = xr * inv * gqr_ref[...]
        xr = xr * cos[:, 0:128] + _swap_quarters(xr, MLA_ROPE // 4) * sin[:, 0:128]
        xr = xr * MLA_SCALE
        halves = (jnp.where(low, xr, 0.0), jnp.where(low, pltpu.roll(xr, MLA_ROPE, 1), 0.0))
        for e in range(2):
            h = 2 * j + e
            qn = _rms_rows(q[:, h * MLA_NOPE:(h + 1) * MLA_NOPE], gqn_ref[...], MLA_NOPE) * MLA_SCALE
            q_ref[0, h, :, 0:MLA_NOPE] = qn.astype(q_ref.dtype)
            q_ref[0, h, :, MLA_NOPE:] = halves[e].astype(q_ref.dtype)

    for h in range(H):
        kn = _rms_rows(kv[:, h * MLA_NOPE:(h + 1) * MLA_NOPE], gkn_ref[...], MLA_NOPE)
        k_ref[0, h, :, 0:MLA_NOPE] = kn.astype(k_ref.dtype)
        k_ref[0, h, :, MLA_NOPE:] = k_rope
        v_ref[0, h] = kv[:, H * MLA_NOPE + h * MLA_V:H * MLA_NOPE + (h + 1) * MLA_V].astype(v_ref.dtype)


def _mla_prep(p3, wq, wkv, q_norm, kv_norm, g_qn, g_qr2, g_kn, g_kr2, cos_t, sin_t):
    B, S, _ = p3.shape
    H = MLA_HEADS
    tm = 256
    full = lambda shape: pl.BlockSpec(shape, lambda b, i: (0,) * len(shape))
    return pl.pallas_call(
        _mla_prep_kernel,
        grid=(B, S // tm),
        in_specs=[pl.BlockSpec((1, tm, 1024), lambda b, i: (b, i, COL_TAIL // 1024)),
                  full(wq.shape), full(wkv.shape),
                  full((1, Q_LORA)), full((1, KV_LORA)), full((1, MLA_NOPE)), full((1, 128)),
                  full((1, MLA_NOPE)), full((1, 128)),
                  pl.BlockSpec((tm, 128), lambda b, i: (i, 0)),
                  pl.BlockSpec((tm, 128), lambda b, i: (i, 0))],
        out_specs=[pl.BlockSpec((1, H, tm, MLA_QK_PAD), lambda b, i: (b, 0, i, 0)),
                   pl.BlockSpec((1, H, tm, MLA_QK_PAD), lambda b, i: (b, 0, i, 0)),
                   pl.BlockSpec((1, H, tm, MLA_V), lambda b, i: (b, 0, i, 0))],
        out_shape=[jax.ShapeDtypeStruct((B, H, S, MLA_QK_PAD), MXU_DT),
                   jax.ShapeDtypeStruct((B, H, S, MLA_QK_PAD), MXU_DT),
                   jax.ShapeDtypeStruct((B, H, S, MLA_V), MXU_DT)],
        compiler_params=_params(("arbitrary", "arbitrary")),
        name="mla_prep",
    )(p3, wq, wkv, q_norm.reshape(1, -1), kv_norm.reshape(1, -1), g_qn.reshape(1, -1), g_qr2,
      g_kn.reshape(1, -1), g_kr2, cos_t, sin_t)


ATT_TQ = 256
ATT_TK = 256


def _attn_kernel(q_ref, k_ref, v_ref, g_ref, o_ref, *, q_off, n_ctx_tiles, n_all_tiles):
    tq, tk = ATT_TQ, ATT_TK
    q = q_ref[0, 0]
    qi = pl.program_id(2) + q_off
    n_kv = jnp.where(qi < n_ctx_tiles, n_ctx_tiles, n_all_tiles)

    def body(j, carry):
        m, l, acc = carry
        ks = pl.multiple_of(j * tk, tk)
        k = k_ref[0, 0, pl.ds(ks, tk), :]
        v = v_ref[0, 0, pl.ds(ks, tk), :]
        s = lax.dot_general(q, k, (((1,), (1,)), ((), ())), preferred_element_type=F32)
        m_new = jnp.maximum(m, jnp.max(s, axis=-1, keepdims=True))
        alpha = jnp.exp(m - m_new)
        p = jnp.exp(s - m_new)
        l = alpha * l + jnp.sum(p, axis=-1, keepdims=True)
        acc = alpha * acc + jnp.dot(p.astype(MXU_DT), v, preferred_element_type=F32)
        return m_new, l, acc

    m0 = jnp.full((tq, 1), -jnp.inf, F32)
    l0 = jnp.zeros((tq, 1), F32)
    a0 = jnp.zeros((tq, MLA_V), F32)
    _, l, acc = lax.fori_loop(0, n_kv, body, (m0, l0, a0))
    o_ref[0] = (acc / l * _silu(g_ref[0])).astype(o_ref.dtype)


def _mla_attention(qc, kc, vc, p3, n_ctx_rows, with_ctx):
    B, H, S, _ = qc.shape
    tq = ATT_TQ
    n_ctx_tiles, n_all_tiles = n_ctx_rows // ATT_TK, S // ATT_TK
    q_off = 0 if with_ctx else n_ctx_rows // tq
    gc0 = COL_GC // MLA_V
    return pl.pallas_call(
        functools.partial(_attn_kernel, q_off=q_off, n_ctx_tiles=n_ctx_tiles, n_all_tiles=n_all_tiles),
        grid=(B, H, S // tq - q_off),
        in_specs=[pl.BlockSpec((1, 1, tq, MLA_QK_PAD), lambda b, h, i: (b, h, i + q_off, 0)),
                  pl.BlockSpec((1, 1, S, MLA_QK_PAD), lambda b, h, i: (b, h, 0, 0)),
                  pl.BlockSpec((1, 1, S, MLA_V), lambda b, h, i: (b, h, 0, 0)),
                  pl.BlockSpec((1, tq, MLA_V), lambda b, h, i: (b, i + q_off, gc0 + h))],
        out_specs=pl.BlockSpec((1, tq, MLA_V), lambda b, h, i: (b, i + q_off, h)),
        out_shape=jax.ShapeDtypeStruct((B, S, H * MLA_V), MXU_DT),
        compiler_params=_params(("arbitrary", "arbitrary", "arbitrary")),
        name="mla_attention",
    )(qc, kc, vc, p3)


def _merge_kernel(ya_ref, yb_ref, yc_ref, ma_ref, mb_ref, mc_ref, x_ref, mod_ref,
                  wa_ref, wb_ref, wc_ref, wo_ref, o_ref):
    def down(y_ref, m_ref, w_ref):
        return jax.nn.sigmoid(m_ref[0]) * jnp.dot(y_ref[0], w_ref[...], preferred_element_type=F32)

    z = down(ya_ref, ma_ref, wa_ref) + down(yb_ref, mb_ref, wb_ref) + down(yc_ref, mc_ref, wc_ref)
    out = jnp.dot(z.astype(MXU_DT), wo_ref[...], preferred_element_type=F32)
    o_ref[0] = x_ref[0] + mod_ref[0, 2:3] * out


def _merge(ya, yb, yc, p3, xs, modrows, wa, wb, wc, wo, row_off):
    B, S, _ = xs.shape
    tm = SUBTILE
    o = row_off // tm
    n_sub = S // tm
    act = pl.BlockSpec((1, tm, D_MODEL), lambda b, i: (b, i + o, 0))
    gm = lambda k: pl.BlockSpec((1, tm, D_MODEL), lambda b, i: (b, i + o, COL_GM // D_MODEL + k))
    wspec = pl.BlockSpec((D_MODEL, D_MODEL), lambda b, i: (0, 0))
    return pl.pallas_call(
        _merge_kernel,
        grid=(B, n_sub - o),
        in_specs=[act, act, act, gm(0), gm(1), gm(2), act,
                  pl.BlockSpec((1, 8, D_MODEL), lambda b, i: (b * n_sub + i + o, 0, 0)),
                  wspec, wspec, wspec, wspec],
        out_specs=pl.BlockSpec((1, tm, D_MODEL), lambda b, i: (b, i, 0)),
        out_shape=jax.ShapeDtypeStruct((B, S - row_off, D_MODEL), F32),
        compiler_params=_params(("arbitrary", "arbitrary")),
        name="merge",
    )(ya, yb, yc, p3, p3, p3, xs, modrows, wa, wb, wc, wo)


def _rope_tables(n_ctx_rows, n_lat_rows, dim):
    rows = n_lat_rows // GRID_W
    row = jnp.repeat(jnp.arange(rows), GRID_W).astype(F32)
    col = jnp.tile(jnp.arange(GRID_W), rows).astype(F32)
    n_freq = dim // 4
    inv = ROPE_BASE ** (-jnp.arange(n_freq, dtype=F32) / n_freq)
    ang_r = row[:, None] * inv[None, :]
    ang_c = col[:, None] * inv[None, :]
    ang = jnp.concatenate([ang_r, ang_r, ang_c, ang_c], axis=-1)
    sign = jnp.tile(jnp.concatenate([-jnp.ones((n_freq,), F32), jnp.ones((n_freq,), F32)]), 2)
    cos = jnp.concatenate([jnp.ones((n_ctx_rows, dim), F32), jnp.cos(ang)], axis=0)
    sin = jnp.concatenate([jnp.zeros((n_ctx_rows, dim), F32), jnp.sin(ang) * sign], axis=0)
    return cos, sin


def _permute_w_in(w):
    xa_to_gb = w[:, :6144]
    tail = w[:, 6144:6848]
    gc_gm = w[:, 6848:]
    pad = jnp.zeros((w.shape[0], P_COLS - w.shape[1]), w.dtype)
    return jnp.concatenate([xa_to_gb, gc_gm, tail, pad], axis=1).astype(MXU_DT)


def _split_heads(w, first):
    K = w.shape[0]
    w3 = w.reshape(K, MLA_HEADS, -1)
    return jnp.concatenate([w3[:, :, :first].reshape(K, -1), w3[:, :, first:].reshape(K, -1)],
                           axis=1).astype(MXU_DT)


def _pad_lanes(g):
    return jnp.concatenate([g, jnp.zeros((128 - g.shape[0],), g.dtype)]).reshape(1, 128)


def kernel(x, c, ctx, c_ctx, w_mod, b_mod, norm_gain, w_in, conv_w, conv_b, lru_wa, lru_ba, lru_wx, lru_bx,
           lru_lambda, ret_theta, ret_gain, mla_q_norm, mla_w_q_up, mla_kv_norm, mla_w_kv_up,
           mla_g_qn, mla_g_qr, mla_g_kn, mla_g_kr, w_down_a, w_down_b, w_down_c, w_out):
    B, T, D = x.shape
    L = ctx.shape[1]
    S = L + T
    depth = w_in.shape[0]
    assert D == D_MODEL and L % SUBTILE == 0 and T % SUBTILE == 0 and T % GRID_W == 0

    cos_r, sin_r = _rope_tables(L, T, RET_HEAD_DIM)
    cos_m, sin_m = _rope_tables(L, T, MLA_ROPE)
    pad64 = lambda t: jnp.concatenate([t, t], axis=1)
    cos_m, sin_m = pad64(cos_m), pad64(sin_m)

    n_rows = ((B + 1 + 7) // 8) * 8
    cc = jnp.zeros((n_rows, D), F32).at[:B].set(c).at[B].set(c_ctx)
    n_sub = S // SUBTILE
    sub = jnp.arange(B * n_sub)
    mod_row = jnp.where(sub % n_sub < L // SUBTILE, B, sub // n_sub)

    xs = jnp.concatenate([ctx, x], axis=1)
    for l in range(depth):
        last = l == depth - 1
        mod = _modulation(cc, w_mod[l], b_mod[l])
        mod3 = mod.reshape(n_rows, 3, D)[mod_row]
        modrows = jnp.concatenate([mod3, jnp.zeros((B * n_sub, 5, D), F32)], axis=1)

        p3 = _inproj(xs.reshape(B * S, D), modrows, norm_gain[l], _permute_w_in(w_in[l])).reshape(B, S, P_COLS)

        w_cat = jnp.concatenate([lru_wa[l], lru_wx[l]], axis=-1).astype(MXU_DT)
        ya = _lru_branch(p3, conv_w[l], conv_b[l], w_cat, lru_ba[l], lru_bx[l], lru_lambda[l], L)
        yb = _ret_branch(p3, cos_r, sin_r, ret_theta[l], ret_gain[l], L)
        qc, kc, vc = _mla_prep(p3, _split_heads(mla_w_q_up[l], MLA_NOPE), _split_heads(mla_w_kv_up[l], MLA_NOPE),
                               mla_q_norm[l], mla_kv_norm[l], mla_g_qn[l],
                               jnp.concatenate([mla_g_qr[l], mla_g_qr[l]]).reshape(1, 128),
                               mla_g_kn[l], _pad_lanes(mla_g_kr[l]), cos_m, sin_m)
        yc = _mla_attention(qc, kc, vc, p3, L, not last)
        xs = _merge(ya, yb, yc, p3, xs, modrows,
                    w_down_a[l].astype(MXU_DT), w_down_b[l].astype(MXU_DT), w_down_c[l].astype(MXU_DT),
                    w_out[l].astype(MXU_DT), L if last else 0)
    return xs
```

```python
import functools
import math

import jax
import jax.numpy as jnp
from jax import lax
from jax.experimental import pallas as pl
from jax.experimental.pallas import tpu as pltpu

D_MODEL = 1024
GRID_W = 64
ROPE_BASE = 10000.0
NORM_EPS = 1e-6

LRU_BLOCKS = 8
LRU_BLOCK_W = D_MODEL // LRU_BLOCKS
CONV_W = 4
LRU_C = 8.0

RET_HEADS = 4
RET_HEAD_DIM = D_MODEL // RET_HEADS
RET_CHUNK = 128

MLA_HEADS = 8
MLA_NOPE = 128
MLA_ROPE = 64
MLA_V = 128
Q_LORA = 384
KV_LORA = 256
MLA_SCALE = (MLA_NOPE + MLA_ROPE) ** -0.5
Q_SCALE = MLA_SCALE * math.log2(math.e)
MLA_V_PAD = 256
MLA_QK_PAD = 256

P_COLS = 11 * 1024
COL_XA, COL_GA, COL_QB, COL_KB, COL_VB, COL_GB, COL_GC, COL_GM, COL_TAIL = (
    0, 1024, 2048, 3072, 4096, 5120, 6144, 7168, 10240)

MXU_DT = jnp.bfloat16
SUBTILE = 256
VMEM_LIMIT = 56 * 1024 * 1024
F32 = jnp.float32


def _params(sem):
    return pltpu.CompilerParams(dimension_semantics=sem, vmem_limit_bytes=VMEM_LIMIT)


def _silu(x):
    return x * jax.nn.sigmoid(x)


def _softplus(x):
    return jnp.maximum(x, 0.0) + jnp.log1p(jnp.exp(-jnp.abs(x)))


def _pick(n, cands):
    for c in cands:
        if n % c == 0:
            return c
    raise ValueError(f"no tile in {cands} divides {n}")


def _mod_kernel(c_ref, w_ref, b_ref, o_ref):
    a = _silu(c_ref[...])
    o_ref[...] = jnp.dot(a, w_ref[...], preferred_element_type=F32) + b_ref[...]


def _modulation(cc, w_mod, b_mod):
    rows = cc.shape[0]
    n3 = w_mod.shape[1]
    tn = 1024
    return pl.pallas_call(
        _mod_kernel,
        grid=(n3 // tn,),
        in_specs=[pl.BlockSpec((rows, D_MODEL), lambda j: (0, 0)),
                  pl.BlockSpec((D_MODEL, tn), lambda j: (0, j)),
                  pl.BlockSpec((1, tn), lambda j: (0, j))],
        out_specs=pl.BlockSpec((rows, tn), lambda j: (0, j)),
        out_shape=jax.ShapeDtypeStruct((rows, n3), F32),
        compiler_params=_params(("arbitrary",)),
        name="modulation",
    )(cc, w_mod, b_mod.reshape(1, n3))


def _inproj_kernel(x_ref, mod_ref, gain_ref, w_ref, o_ref, h_ref, *, tm):
    @pl.when(pl.program_id(1) == 0)
    def _():
        gain = gain_ref[...]
        for s in range(tm // SUBTILE):
            rows = slice(s * SUBTILE, (s + 1) * SUBTILE)
            x = x_ref[rows, :]
            y = x * lax.rsqrt(jnp.mean(x * x, axis=-1, keepdims=True) + NORM_EPS) * gain
            h_ref[rows, :] = (y * (1.0 + mod_ref[s, 1:2, :]) + mod_ref[s, 0:1, :]).astype(h_ref.dtype)

    o_ref[...] = jnp.dot(h_ref[...], w_ref[...], preferred_element_type=F32)


def _inproj(xs2, modrows, gain, w_in_p):
    n = xs2.shape[0]
    tm = _pick(n, (1024, 512, 256))
    tn = 1024
    return pl.pallas_call(
        functools.partial(_inproj_kernel, tm=tm),
        grid=(n // tm, P_COLS // tn),
        in_specs=[pl.BlockSpec((tm, D_MODEL), lambda i, j: (i, 0)),
                  pl.BlockSpec((tm // SUBTILE, 8, D_MODEL), lambda i, j: (i, 0, 0)),
                  pl.BlockSpec((1, D_MODEL), lambda i, j: (0, 0)),
                  pl.BlockSpec((D_MODEL, tn), lambda i, j: (0, j))],
        out_specs=pl.BlockSpec((tm, tn), lambda i, j: (i, j)),
        out_shape=jax.ShapeDtypeStruct((n, P_COLS), F32),
        scratch_shapes=[pltpu.VMEM((tm, D_MODEL), MXU_DT)],
        compiler_params=_params(("arbitrary", "arbitrary")),
        name="inproj",
    )(xs2, modrows, gain.reshape(1, D_MODEL), w_in_p)


LRU_CHUNK = 128
LRU_NB = 2
SCAN_SEGS = 8


def _lru_kernel(xa_ref, ga_ref, cw_ref, cb_ref, w_ref, ba_ref, bx_ref, lam_ref,
                o_ref, u_s, hf_s, xe_s, as_s, bs_s, hs_s, ac_s, ho_s, e_s, *, n_ctx, n_lat):
    C = LRU_CHUNK
    W = LRU_BLOCK_W
    WB = LRU_NB * W
    SEG = C // SCAN_SEGS
    S = (n_ctx + n_lat) * C
    cw = cw_ref[...]
    cb = cb_ref[...]
    sub = lax.broadcasted_iota(jnp.int32, (SCAN_SEGS, W), 0)

    def conv_chunk(start, first, last):
        x = xa_ref[0, pl.ds(start, C), :]
        prev = xa_ref[0, pl.ds(pl.multiple_of(jnp.maximum(start - 8, 0), 8), 8), :]
        nxt = xa_ref[0, pl.ds(pl.multiple_of(jnp.minimum(start + C, S - 8), 8), 8), :]
        xe_s[0:8, :] = jnp.where(first, 0.0, prev)
        xe_s[8:8 + C, :] = x
        xe_s[8 + C:, :] = jnp.where(last, 0.0, nxt)
        return (xe_s[6:6 + C, :] * cw[0:1] + xe_s[7:7 + C, :] * cw[1:2]
                + x * cw[2:3] + xe_s[9:9 + C, :] * cw[3:4] + cb)

    def coeffs(u, d):
        um = u.astype(MXU_DT)
        z = [jnp.dot(um[:, n * W:(n + 1) * W], w_ref[d, n], preferred_element_type=F32) for n in range(LRU_NB)]
        za = jnp.concatenate([zn[:, :W] for zn in z], axis=1)
        zx = jnp.concatenate([zn[:, W:] for zn in z], axis=1)
        r = jax.nn.sigmoid(za + ba_ref[d:d + 1, :])
        i = jax.nn.sigmoid(zx + bx_ref[d:d + 1, :])
        log_a = (-LRU_C * r) * _softplus(-lam_ref[d:d + 1, :])
        a = jnp.exp(log_a)
        g = -jnp.tanh(log_a) * (1.0 + a * a)
        root = jnp.where(g > 0.0, g * lax.rsqrt(g), 0.0)
        return a, root * (i * u)

    def scan_block(n, h, reverse):
        hk = ak = None
        for k in (range(SEG - 1, -1, -1) if reverse else range(SEG)):
            a_k = as_s[n, pl.ds(k, SCAN_SEGS, stride=SEG), :]
            b_k = bs_s[n, pl.ds(k, SCAN_SEGS, stride=SEG), :]
            if hk is None:
                hk, ak = b_k, a_k
            else:
                hk = a_k * hk + b_k
                ak = a_k * ak
            hs_s[n, k] = hk
            ac_s[n, k] = ak
        s = 1
        while s < SCAN_SEGS:
            shift = SCAN_SEGS - s if reverse else s
            m = sub < SCAN_SEGS - s if reverse else sub >= s
            h_o = pltpu.roll(hk, shift, 0)
            a_o = pltpu.roll(ak, shift, 0)
            hk = jnp.where(m, ak * h_o + hk, hk)
            ak = jnp.where(m, ak * a_o, ak)
            s *= 2
        end = hk + ak * h
        e_s[n] = end
        if reverse:
            cin = jnp.where(sub == SCAN_SEGS - 1, h, pltpu.roll(end, SCAN_SEGS - 1, 0))
            h_new = e_s[n, 0:1, :]
        else:
            cin = jnp.where(sub == 0, h, pltpu.roll(end, 1, 0))
            h_new = e_s[n, SCAN_SEGS - 1:SCAN_SEGS, :]
        for k in range(SEG):
            ho_s[n, pl.ds(k, SCAN_SEGS, stride=SEG), :] = hs_s[n, k] + ac_s[n, k] * cin
        return h_new

    def scan_chunk(a, b, h, reverse):
        for n in range(LRU_NB):
            as_s[n] = a[:, n * W:(n + 1) * W]
            bs_s[n] = b[:, n * W:(n + 1) * W]
        h_new = [scan_block(n, h[:, n * W:(n + 1) * W], reverse) for n in range(LRU_NB)]
        return (jnp.concatenate([ho_s[n] for n in range(LRU_NB)], axis=1),
                jnp.concatenate(h_new, axis=1))

    zero_h = jnp.zeros((1, WB), F32)

    def fwd_body(c, h):
        start = pl.multiple_of(c * C, C)
        first = jnp.logical_or(c == 0, c == n_ctx)
        last = jnp.logical_or(c == n_ctx - 1, c == n_ctx + n_lat - 1)
        u = conv_chunk(start, first, last)
        u_s[pl.ds(start, C), :] = u
        a, b = coeffs(u, 0)
        hfull, h = scan_chunk(a, b, h, False)
        hf_s[pl.ds(start, C), :] = hfull
        return h

    lax.fori_loop(0, n_ctx + n_lat, fwd_body, zero_h)

    def bwd_body(c, h):
        start = pl.multiple_of(c * C, C)
        u = u_s[pl.ds(start, C), :]
        a, b = coeffs(u, 1)
        hfull, h = scan_chunk(a, b, h, True)
        y = hf_s[pl.ds(start, C), :] + hfull
        o_ref[0, pl.ds(start, C), :] = (y * _silu(ga_ref[0, pl.ds(start, C), :])).astype(o_ref.dtype)
        return h

    h = lax.fori_loop(0, n_ctx, lambda i, h: bwd_body(n_ctx - 1 - i, h), zero_h)
    lax.fori_loop(0, n_lat, lambda i, h: bwd_body(n_ctx + n_lat - 1 - i, h), h)


def _lru_branch(p3, conv_w, conv_b, w_cat, ba, bx, lam, n_ctx_rows):
    B, S, _ = p3.shape
    C, W, NB = LRU_CHUNK, LRU_BLOCK_W, LRU_NB
    WB = NB * W
    n_ctx, n_lat = n_ctx_rows // C, (S - n_ctx_rows) // C
    xa0, ga0 = COL_XA // WB, COL_GA // WB
    per_block = lambda rows: pl.BlockSpec((rows, WB), lambda b, n: (0, n))
    return pl.pallas_call(
        functools.partial(_lru_kernel, n_ctx=n_ctx, n_lat=n_lat),
        grid=(B, LRU_BLOCKS // NB),
        in_specs=[pl.BlockSpec((1, S, WB), lambda b, n: (b, 0, xa0 + n)),
                  pl.BlockSpec((1, S, WB), lambda b, n: (b, 0, ga0 + n)),
                  per_block(CONV_W), per_block(1),
                  pl.BlockSpec((2, NB, W, 2 * W), lambda b, n: (0, n, 0, 0)),
                  per_block(2), per_block(2), per_block(2)],
        out_specs=pl.BlockSpec((1, S, WB), lambda b, n: (b, 0, n)),
        out_shape=jax.ShapeDtypeStruct((B, S, D_MODEL), MXU_DT),
        scratch_shapes=[pltpu.VMEM((S, WB), F32), pltpu.VMEM((S, WB), F32),
                        pltpu.VMEM((C + 16, WB), F32), pltpu.VMEM((NB, C, W), F32), pltpu.VMEM((NB, C, W), F32),
                        pltpu.VMEM((NB, C // SCAN_SEGS, SCAN_SEGS, W), F32),
                        pltpu.VMEM((NB, C // SCAN_SEGS, SCAN_SEGS, W), F32),
                        pltpu.VMEM((NB, C, W), F32), pltpu.VMEM((NB, SCAN_SEGS, W), F32)],
        compiler_params=_params(("arbitrary", "arbitrary")),
        name="lru",
    )(p3, p3, conv_w, conv_b.reshape(1, D_MODEL), w_cat, ba, bx, lam)


def _swap_quarters(x, q):
    lane = lax.broadcasted_iota(jnp.int32, (x.shape[0], 128), 1)
    cols = []
    for c in range(x.shape[1] // 128):
        xc = x[:, c * 128:(c + 1) * 128]
        if 2 * q == 128:
            cols.append(pltpu.roll(xc, q, 1))
        else:
            cols.append(jnp.where(lane % (2 * q) < q, pltpu.roll(xc, 128 - q, 1), pltpu.roll(xc, q, 1)))
    return cols[0] if len(cols) == 1 else jnp.concatenate(cols, axis=1)


def _ret_chunk_index(d, c, n_ctx, n_all):
    rev = jnp.where(c < n_ctx, n_ctx - 1 - c, n_ctx + n_all - 1 - c)
    return jnp.where(d == 0, c, rev)


def _ret_kernel(q_ref, k_ref, v_ref, g_ref, cos_ref, sin_ref, theta_ref, gain_ref,
                o_ref, s_s, of_s, oc_s, inner_s, dec_s, *, n_ctx, n_all):
    C, dh = RET_CHUNK, RET_HEAD_DIM
    d = pl.program_id(1)
    c = pl.program_id(2)
    ci = _ret_chunk_index(d, c, n_ctx, n_all)
    start = pl.multiple_of(ci * C, C)
    fwd = d == 0
    wide = lambda t: jnp.concatenate([t] * (dh // C), axis=1)

    @pl.when(c == 0)
    def _():
        s_s[...] = jnp.zeros_like(s_s)
        row = lax.broadcasted_iota(jnp.int32, (C, C), 0)
        col = lax.broadcasted_iota(jnp.int32, (C, C), 1)
        diff = jnp.where(fwd, row - col, col - row).astype(F32)
        q_pow = jnp.where(fwd, row + 1, C - row).astype(F32)
        k_pow = jnp.where(fwd, C - 1 - row, row).astype(F32)
        for h in range(RET_HEADS):
            lg = -_softplus(-theta_ref[pl.ds(d * RET_HEADS + h, 1), :])
            inner_s[h] = jnp.where(diff >= 0, jnp.exp(lg * jnp.maximum(diff, 0.0)), 0.0)
            dec_s[h, 0] = jnp.exp(lg * q_pow)
            dec_s[h, 1] = jnp.exp(lg * k_pow)
            dec_s[h, 2] = jnp.broadcast_to(jnp.exp(lg * float(C)), (C, C))

    cos = cos_ref[...]
    sin = sin_ref[...]
    k_scale = dh ** -0.5
    for h in range(RET_HEADS):
        cs = slice(h * dh, (h + 1) * dh)
        q = q_ref[0, :, cs]
        k = k_ref[0, :, cs]
        q = q * cos + _swap_quarters(q, dh // 4) * sin
        k = (k * cos + _swap_quarters(k, dh // 4) * sin) * k_scale
        qm = q.astype(MXU_DT)
        km = k.astype(MXU_DT)
        vm = v_ref[0, :, cs].astype(MXU_DT)
        s_old = s_s[h]
        att = lax.dot_general(qm, km, (((1,), (1,)), ((), ())), preferred_element_type=F32) * inner_s[h]
        oc_s[:, cs] = (jnp.dot(att.astype(MXU_DT), vm, preferred_element_type=F32)
                       + jnp.dot(qm, s_old.astype(MXU_DT), preferred_element_type=F32) * wide(dec_s[h, 0]))
        kt = jnp.transpose(k * wide(dec_s[h, 1])).astype(MXU_DT)
        s_s[h] = s_old * wide(dec_s[h, 2, 0:1, :]) + jnp.dot(kt, vm, preferred_element_type=F32)

    @pl.when(fwd)
    def _():
        of_s[pl.ds(start, C), :] = oc_s[...]

    @pl.when(jnp.logical_not(fwd))
    def _():
        for h in range(RET_HEADS):
            cs = slice(h * dh, (h + 1) * dh)
            t = oc_s[:, cs] + of_s[pl.ds(start, C), cs]
            tc = t - jnp.mean(t, axis=-1, keepdims=True)
            y = tc * lax.rsqrt(jnp.mean(tc * tc, axis=-1, keepdims=True) + NORM_EPS) * gain_ref[:, cs]
            o_ref[0, :, cs] = (y * _silu(g_ref[0, :, cs])).astype(o_ref.dtype)


def _ret_branch(p3, cos_t, sin_t, theta, gain, n_ctx_rows):
    B, S, _ = p3.shape
    C = RET_CHUNK
    n_ctx, n_all = n_ctx_rows // C, S // C
    cidx = functools.partial(_ret_chunk_index, n_ctx=n_ctx, n_all=n_all)

    def pcol(col):
        return pl.BlockSpec((1, C, D_MODEL), lambda b, d, c: (b, cidx(d, c), col // D_MODEL))

    def out_map(b, d, c):
        return (b, jnp.where(d == 0, cidx(1, 0), cidx(d, c)), 0)

    return pl.pallas_call(
        functools.partial(_ret_kernel, n_ctx=n_ctx, n_all=n_all),
        grid=(B, 2, n_all),
        in_specs=[pcol(COL_QB), pcol(COL_KB), pcol(COL_VB), pcol(COL_GB),
                  pl.BlockSpec((C, RET_HEAD_DIM), lambda b, d, c: (cidx(d, c), 0)),
                  pl.BlockSpec((C, RET_HEAD_DIM), lambda b, d, c: (cidx(d, c), 0)),
                  pl.BlockSpec((2 * RET_HEADS, C), lambda b, d, c: (0, 0)),
                  pl.BlockSpec((1, D_MODEL), lambda b, d, c: (0, 0))],
        out_specs=pl.BlockSpec((1, C, D_MODEL), out_map),
        out_shape=jax.ShapeDtypeStruct((B, S, D_MODEL), MXU_DT),
        scratch_shapes=[pltpu.VMEM((RET_HEADS, RET_HEAD_DIM, RET_HEAD_DIM), F32),
                        pltpu.VMEM((S, D_MODEL), F32), pltpu.VMEM((C, D_MODEL), F32),
                        pltpu.VMEM((RET_HEADS, C, C), F32), pltpu.VMEM((RET_HEADS, 3, C, C), F32)],
        compiler_params=_params(("arbitrary", "arbitrary", "arbitrary")),
        name="retention",
    )(p3, p3, p3, p3, cos_t, sin_t, jnp.broadcast_to(theta.reshape(2 * RET_HEADS, 1), (2 * RET_HEADS, C)),
      gain.reshape(1, D_MODEL))


def _rms_rows(x, gain, width):
    return x * lax.rsqrt(jnp.sum(x * x, axis=-1, keepdims=True) * (1.0 / width) + NORM_EPS) * gain


def _mla_prep_kernel(p_ref, wq_ref, wkv_ref, qn_ref, kvn_ref, gqn_ref, gqr_ref, gkn_ref, gkr_ref,
                     cos_ref, sin_ref, q_ref, k_ref, v_ref):
    H = MLA_HEADS
    blk = p_ref[0]
    qd = blk[:, 0:Q_LORA]
    kvd = blk[:, Q_LORA:Q_LORA + KV_LORA]
    kr = blk[:, Q_LORA + KV_LORA:Q_LORA + KV_LORA + 128]
    tm = blk.shape[0]
    lane = lax.broadcasted_iota(jnp.int32, (tm, 128), 1)
    low = lane < MLA_ROPE
    cos = cos_ref[...]
    sin = sin_ref[...]

    q = jnp.dot(_rms_rows(qd, qn_ref[...], Q_LORA).astype(MXU_DT), wq_ref[...], preferred_element_type=F32)
    kv = jnp.dot(_rms_rows(kvd, kvn_ref[...], KV_LORA).astype(MXU_DT), wkv_ref[...], preferred_element_type=F32)

    k_rope = _rms_rows(kr, gkr_ref[...], MLA_ROPE)
    k_rope = k_rope * cos[:, 0:128] + _swap_quarters(k_rope, MLA_ROPE // 4) * sin[:, 0:128]
    k_rope = jnp.where(low, k_rope, 0.0)

    for j in range(H // 2):
        xr = q[:, H * MLA_NOPE + j * 128:H * MLA_NOPE + (j + 1) * 128]
        sq = xr * xr
        ms_lo = jnp.sum(jnp.where(low, sq, 0.0), axis=-1, keepdims=True)
        ms_hi = jnp.sum(jnp.where(low, 0.0, sq), axis=-1, keepdims=True)
        inv = lax.rsqrt(jnp.where(low, ms_lo, ms_hi) * (1.0 / MLA_ROPE) + NORM_EPS)
        xr = xr * inv * gqr_ref[...]
        xr = xr * cos[:, 0:128] + _swap_quarters(xr, MLA_ROPE // 4) * sin[:, 0:128]
        xr = xr * Q_SCALE
        halves = (jnp.where(low, xr, 0.0), jnp.where(low, pltpu.roll(xr, MLA_ROPE, 1), 0.0))
        for e in range(2):
            h = 2 * j + e
            qn = _rms_rows(q[:, h * MLA_NOPE:(h + 1) * MLA_NOPE], gqn_ref[...], MLA_NOPE) * Q_SCALE
            q_ref[0, h, :, 0:MLA_NOPE] = qn.astype(q_ref.dtype)
            q_ref[0, h, :, MLA_NOPE:] = halves[e].astype(q_ref.dtype)

    for h in range(H):
        kn = _rms_rows(kv[:, h * MLA_NOPE:(h + 1) * MLA_NOPE], gkn_ref[...], MLA_NOPE)
        k_ref[0, h, 0:MLA_NOPE, :] = jnp.transpose(kn).astype(k_ref.dtype)
        k_ref[0, h, MLA_NOPE:, :] = jnp.transpose(k_rope).astype(k_ref.dtype)
        v_ref[0, h, :, 0:MLA_V] = kv[:, H * MLA_NOPE + h * MLA_V:H * MLA_NOPE + (h + 1) * MLA_V].astype(v_ref.dtype)
        v_ref[0, h, :, MLA_V:] = jnp.where(lane == 0, 1.0, 0.0).astype(v_ref.dtype)


def _mla_prep(p3, wq, wkv, q_norm, kv_norm, g_qn, g_qr2, g_kn, g_kr2, cos_t, sin_t):
    B, S, _ = p3.shape
    H = MLA_HEADS
    tm = 256
    full = lambda shape: pl.BlockSpec(shape, lambda b, i: (0,) * len(shape))
    return pl.pallas_call(
        _mla_prep_kernel,
        grid=(B, S // tm),
        in_specs=[pl.BlockSpec((1, tm, 1024), lambda b, i: (b, i, COL_TAIL // 1024)),
                  full(wq.shape), full(wkv.shape),
                  full((1, Q_LORA)), full((1, KV_LORA)), full((1, MLA_NOPE)), full((1, 128)),
                  full((1, MLA_NOPE)), full((1, 128)),
                  pl.BlockSpec((tm, 128), lambda b, i: (i, 0)),
                  pl.BlockSpec((tm, 128), lambda b, i: (i, 0))],
        out_specs=[pl.BlockSpec((1, H, tm, MLA_QK_PAD), lambda b, i: (b, 0, i, 0)),
                   pl.BlockSpec((1, H, MLA_QK_PAD, tm), lambda b, i: (b, 0, 0, i)),
                   pl.BlockSpec((1, H, tm, MLA_V_PAD), lambda b, i: (b, 0, i, 0))],
        out_shape=[jax.ShapeDtypeStruct((B, H, S, MLA_QK_PAD), MXU_DT),
                   jax.ShapeDtypeStruct((B, H, MLA_QK_PAD, S), MXU_DT),
                   jax.ShapeDtypeStruct((B, H, S, MLA_V_PAD), MXU_DT)],
        compiler_params=_params(("arbitrary", "arbitrary")),
        name="mla_prep",
    )(p3, wq, wkv, q_norm.reshape(1, -1), kv_norm.reshape(1, -1), g_qn.reshape(1, -1), g_qr2,
      g_kn.reshape(1, -1), g_kr2, cos_t, sin_t)


ATT_TQ = 256
ATT_TK = 256


def _attn_kernel(q_ref, qn_ref, kt_ref, v_ref, g_ref, o_ref, s_s, m_s, *, n_first, n_all_tiles):
    tq, tk = ATT_TQ, ATT_TK
    i = pl.program_id(2)

    def scores(q, j):
        s = jnp.dot(q, kt_ref[0, 0, :, j * tk:(j + 1) * tk], preferred_element_type=F32)
        s_s[:, j * tk:(j + 1) * tk] = s
        mm = s[:, 0:128]
        for c in range(1, tk // 128):
            mm = jnp.maximum(mm, s[:, c * 128:(c + 1) * 128])
        return mm

    def set_row_max(mm):
        m_s[...] = jnp.broadcast_to(jnp.max(mm, axis=-1, keepdims=True), (tq, 128))

    def stage(q, n_kv):
        mm = scores(q, 0)
        for j in range(1, n_kv):
            mm = jnp.maximum(mm, scores(q, j))
        set_row_max(mm)

    def attend(n_cur, n_next):
        m = m_s[...]
        qn = qn_ref[0, 0]
        acc = jnp.zeros((tq, MLA_V_PAD), F32)
        mm = None
        for j in range(max(n_cur, n_next)):
            if j < n_cur:
                ps = [jnp.exp2(s_s[:, j * tk + c * 128:j * tk + (c + 1) * 128] - m).astype(MXU_DT)
                      for c in range(tk // 128)]
                acc = acc + jnp.dot(jnp.concatenate(ps, axis=1), v_ref[0, 0, j * tk:(j + 1) * tk, :],
                                    preferred_element_type=F32)
            if j < n_next:
                c = scores(qn, j)
                mm = c if mm is None else jnp.maximum(mm, c)
        set_row_max(mm)
        l = jnp.sum(acc[:, MLA_V:], axis=-1, keepdims=True)
        o_ref[0] = (acc[:, :MLA_V] / l * _silu(g_ref[0])).astype(o_ref.dtype)

    pl.when(i == 0)(lambda: stage(q_ref[0, 0], n_first))
    if n_first == n_all_tiles:
        attend(n_all_tiles, n_all_tiles)
    else:
        pl.when(i == 0)(lambda: attend(n_first, n_all_tiles))
        pl.when(i > 0)(lambda: attend(n_all_tiles, n_all_tiles))


def _mla_attention(qc, kt, vc, p3, n_ctx_rows, with_ctx):
    B, H, S, _ = qc.shape
    tq = ATT_TQ
    assert n_ctx_rows == tq and ATT_TK == tq
    n_all_tiles = S // ATT_TK
    q_off = 0 if with_ctx else 1
    n_q = S // tq - q_off
    n_first = n_ctx_rows // ATT_TK if with_ctx else n_all_tiles
    gc0 = COL_GC // MLA_V
    return pl.pallas_call(
        functools.partial(_attn_kernel, n_first=n_first, n_all_tiles=n_all_tiles),
        grid=(B, H, n_q),
        in_specs=[pl.BlockSpec((1, 1, tq, MLA_QK_PAD), lambda b, h, i: (b, h, i + q_off, 0)),
                  pl.BlockSpec((1, 1, tq, MLA_QK_PAD),
                               lambda b, h, i: (b, h, jnp.minimum(i + 1, n_q - 1) + q_off, 0)),
                  pl.BlockSpec((1, 1, MLA_QK_PAD, S), lambda b, h, i: (b, h, 0, 0)),
                  pl.BlockSpec((1, 1, S, MLA_V_PAD), lambda b, h, i: (b, h, 0, 0)),
                  pl.BlockSpec((1, tq, MLA_V), lambda b, h, i: (b, i + q_off, gc0 + h))],
        out_specs=pl.BlockSpec((1, tq, MLA_V), lambda b, h, i: (b, i + q_off, h)),
        out_shape=jax.ShapeDtypeStruct((B, S, H * MLA_V), MXU_DT),
        scratch_shapes=[pltpu.VMEM((tq, S), F32), pltpu.VMEM((tq, 128), F32)],
        compiler_params=_params(("arbitrary", "arbitrary", "arbitrary")),
        name="mla_attention",
    )(qc, qc, kt, vc, p3)


def _merge_kernel(ya_ref, yb_ref, yc_ref, ma_ref, mb_ref, mc_ref, x_ref, mod_ref,
                  wa_ref, wb_ref, wc_ref, wo_ref, o_ref):
    def down(y_ref, m_ref, w_ref):
        return jax.nn.sigmoid(m_ref[0]) * jnp.dot(y_ref[0], w_ref[...], preferred_element_type=F32)

    z = down(ya_ref, ma_ref, wa_ref) + down(yb_ref, mb_ref, wb_ref) + down(yc_ref, mc_ref, wc_ref)
    out = jnp.dot(z.astype(MXU_DT), wo_ref[...], preferred_element_type=F32)
    o_ref[0] = x_ref[0] + mod_ref[0, 2:3] * out


def _merge(ya, yb, yc, p3, xs, modrows, wa, wb, wc, wo, row_off):
    B, S, _ = xs.shape
    tm = SUBTILE
    o = row_off // tm
    n_sub = S // tm
    act = pl.BlockSpec((1, tm, D_MODEL), lambda b, i: (b, i + o, 0))
    gm = lambda k: pl.BlockSpec((1, tm, D_MODEL), lambda b, i: (b, i + o, COL_GM // D_MODEL + k))
    wspec = pl.BlockSpec((D_MODEL, D_MODEL), lambda b, i: (0, 0))
    return pl.pallas_call(
        _merge_kernel,
        grid=(B, n_sub - o),
        in_specs=[act, act, act, gm(0), gm(1), gm(2), act,
                  pl.BlockSpec((1, 8, D_MODEL), lambda b, i: (b * n_sub + i + o, 0, 0)),
                  wspec, wspec, wspec, wspec],
        out_specs=pl.BlockSpec((1, tm, D_MODEL), lambda b, i: (b, i, 0)),
        out_shape=jax.ShapeDtypeStruct((B, S - row_off, D_MODEL), F32),
        compiler_params=_params(("arbitrary", "arbitrary")),
        name="merge",
    )(ya, yb, yc, p3, p3, p3, xs, modrows, wa, wb, wc, wo)


def _rope_tables(n_ctx_rows, n_lat_rows, dim):
    rows = n_lat_rows // GRID_W
    row = jnp.repeat(jnp.arange(rows), GRID_W).astype(F32)
    col = jnp.tile(jnp.arange(GRID_W), rows).astype(F32)
    n_freq = dim // 4
    inv = ROPE_BASE ** (-jnp.arange(n_freq, dtype=F32) / n_freq)
    ang_r = row[:, None] * inv[None, :]
    ang_c = col[:, None] * inv[None, :]
    ang = jnp.concatenate([ang_r, ang_r, ang_c, ang_c], axis=-1)
    sign = jnp.tile(jnp.concatenate([-jnp.ones((n_freq,), F32), jnp.ones((n_freq,), F32)]), 2)
    cos = jnp.concatenate([jnp.ones((n_ctx_rows, dim), F32), jnp.cos(ang)], axis=0)
    sin = jnp.concatenate([jnp.zeros((n_ctx_rows, dim), F32), jnp.sin(ang) * sign], axis=0)
    return cos, sin


def _permute_w_in(w):
    xa_to_gb = w[:, :6144]
    tail = w[:, 6144:6848]
    gc_gm = w[:, 6848:]
    pad = jnp.zeros((w.shape[0], P_COLS - w.shape[1]), w.dtype)
    return jnp.concatenate([xa_to_gb, gc_gm, tail, pad], axis=1).astype(MXU_DT)


def _split_heads(w, first):
    K = w.shape[0]
    w3 = w.reshape(K, MLA_HEADS, -1)
    return jnp.concatenate([w3[:, :, :first].reshape(K, -1), w3[:, :, first:].reshape(K, -1)],
                           axis=1).astype(MXU_DT)


def _pad_lanes(g):
    return jnp.concatenate([g, jnp.zeros((128 - g.shape[0],), g.dtype)]).reshape(1, 128)


def kernel(x, c, ctx, c_ctx, w_mod, b_mod, norm_gain, w_in, conv_w, conv_b, lru_wa, lru_ba, lru_wx, lru_bx,
           lru_lambda, ret_theta, ret_gain, mla_q_norm, mla_w_q_up, mla_kv_norm, mla_w_kv_up,
           mla_g_qn, mla_g_qr, mla_g_kn, mla_g_kr, w_down_a, w_down_b, w_down_c, w_out):
    B, T, D = x.shape
    L = ctx.shape[1]
    S = L + T
    depth = w_in.shape[0]
    assert D == D_MODEL and L % SUBTILE == 0 and T % SUBTILE == 0 and T % GRID_W == 0

    cos_r, sin_r = _rope_tables(L, T, RET_HEAD_DIM)
    cos_m, sin_m = _rope_tables(L, T, MLA_ROPE)
    pad64 = lambda t: jnp.concatenate([t, t], axis=1)
    cos_m, sin_m = pad64(cos_m), pad64(sin_m)

    n_rows = ((B + 1 + 7) // 8) * 8
    cc = jnp.zeros((n_rows, D), F32).at[:B].set(c).at[B].set(c_ctx)
    n_sub = S // SUBTILE
    sub = jnp.arange(B * n_sub)
    mod_row = jnp.where(sub % n_sub < L // SUBTILE, B, sub // n_sub)

    xs = jnp.concatenate([ctx, x], axis=1)
    for l in range(depth):
        last = l == depth - 1
        mod = _modulation(cc, w_mod[l], b_mod[l])
        mod3 = mod.reshape(n_rows, 3, D)[mod_row]
        modrows = jnp.concatenate([mod3, jnp.zeros((B * n_sub, 5, D), F32)], axis=1)

        p3 = _inproj(xs.reshape(B * S, D), modrows, norm_gain[l], _permute_w_in(w_in[l])).reshape(B, S, P_COLS)

        w_cat = jnp.concatenate([lru_wa[l], lru_wx[l]], axis=-1).astype(MXU_DT)
        ya = _lru_branch(p3, conv_w[l], conv_b[l], w_cat, lru_ba[l], lru_bx[l], lru_lambda[l], L)
        yb = _ret_branch(p3, cos_r, sin_r, ret_theta[l], ret_gain[l], L)
        qc, kc, vc = _mla_prep(p3, _split_heads(mla_w_q_up[l], MLA_NOPE), _split_heads(mla_w_kv_up[l], MLA_NOPE),
                               mla_q_norm[l], mla_kv_norm[l], mla_g_qn[l],
                               jnp.concatenate([mla_g_qr[l], mla_g_qr[l]]).reshape(1, 128),
                               mla_g_kn[l], _pad_lanes(mla_g_kr[l]), cos_m, sin_m)
        yc = _mla_attention(qc, kc, vc, p3, L, not last)
        xs = _merge(ya, yb, yc, p3, xs, modrows,
                    w_down_a[l].astype(MXU_DT), w_down_b[l].astype(MXU_DT), w_down_c[l].astype(MXU_DT),
                    w_out[l].astype(MXU_DT), L if last else 0)
    return xs
```

```python
import functools
import math

import jax
import jax.numpy as jnp
from jax import lax
from jax.experimental import pallas as pl
from jax.experimental.pallas import tpu as pltpu

D_MODEL = 1024
GRID_W = 64
ROPE_BASE = 10000.0
NORM_EPS = 1e-6

LRU_BLOCKS = 8
LRU_BLOCK_W = D_MODEL // LRU_BLOCKS
CONV_W = 4
LRU_C = 8.0

RET_HEADS = 4
RET_HEAD_DIM = D_MODEL // RET_HEADS
RET_CHUNK = 128

MLA_HEADS = 8
MLA_NOPE = 128
MLA_ROPE = 64
MLA_V = 128
Q_LORA = 384
KV_LORA = 256
MLA_SCALE = (MLA_NOPE + MLA_ROPE) ** -0.5
Q_SCALE = MLA_SCALE * math.log2(math.e)
MLA_V_PAD = 256
MLA_QK_PAD = 256

P_COLS = 11 * 1024
COL_XA, COL_GA, COL_QB, COL_KB, COL_VB, COL_GB, COL_GC, COL_GM, COL_TAIL = (
    0, 1024, 2048, 3072, 4096, 5120, 6144, 7168, 10240)

MXU_DT = jnp.bfloat16
P_DT = jnp.bfloat16
SUBTILE = 256
VMEM_LIMIT = 56 * 1024 * 1024
F32 = jnp.float32


def _params(sem):
    return pltpu.CompilerParams(dimension_semantics=sem, vmem_limit_bytes=VMEM_LIMIT)


def _silu(x):
    return x * jax.nn.sigmoid(x)


def _softplus(x):
    return jnp.maximum(x, 0.0) + jnp.log1p(jnp.exp(-jnp.abs(x)))


def _pick(n, cands):
    for c in cands:
        if n % c == 0:
            return c
    raise ValueError(f"no tile in {cands} divides {n}")


def _mod_kernel(c_ref, w_ref, b_ref, o_ref):
    a = _silu(c_ref[...])
    o_ref[...] = jnp.dot(a, w_ref[...], preferred_element_type=F32) + b_ref[...]


def _modulation(cc, w_mod, b_mod):
    rows = cc.shape[0]
    n3 = w_mod.shape[1]
    tn = 1024
    return pl.pallas_call(
        _mod_kernel,
        grid=(n3 // tn,),
        in_specs=[pl.BlockSpec((rows, D_MODEL), lambda j: (0, 0)),
                  pl.BlockSpec((D_MODEL, tn), lambda j: (0, j)),
                  pl.BlockSpec((1, tn), lambda j: (0, j))],
        out_specs=pl.BlockSpec((rows, tn), lambda j: (0, j)),
        out_shape=jax.ShapeDtypeStruct((rows, n3), F32),
        compiler_params=_params(("arbitrary",)),
        name="modulation",
    )(cc, w_mod, b_mod.reshape(1, n3))


def _inproj_kernel(x_ref, mod_ref, gain_ref, w_ref, o_ref, h_ref, *, tm):
    @pl.when(pl.program_id(1) == 0)
    def _():
        gain = gain_ref[...]
        for s in range(tm // SUBTILE):
            rows = slice(s * SUBTILE, (s + 1) * SUBTILE)
            x = x_ref[rows, :]
            y = x * lax.rsqrt(jnp.mean(x * x, axis=-1, keepdims=True) + NORM_EPS) * gain
            h_ref[rows, :] = (y * (1.0 + mod_ref[s, 1:2, :]) + mod_ref[s, 0:1, :]).astype(h_ref.dtype)

    o_ref[...] = jnp.dot(h_ref[...], w_ref[...], preferred_element_type=F32).astype(o_ref.dtype)


def _inproj(xs2, modrows, gain, w_in_p):
    n = xs2.shape[0]
    tm = _pick(n, (1024, 512, 256))
    tn = 1024
    return pl.pallas_call(
        functools.partial(_inproj_kernel, tm=tm),
        grid=(n // tm, P_COLS // tn),
        in_specs=[pl.BlockSpec((tm, D_MODEL), lambda i, j: (i, 0)),
                  pl.BlockSpec((tm // SUBTILE, 8, D_MODEL), lambda i, j: (i, 0, 0)),
                  pl.BlockSpec((1, D_MODEL), lambda i, j: (0, 0)),
                  pl.BlockSpec((D_MODEL, tn), lambda i, j: (0, j))],
        out_specs=pl.BlockSpec((tm, tn), lambda i, j: (i, j)),
        out_shape=jax.ShapeDtypeStruct((n, P_COLS), P_DT),
        scratch_shapes=[pltpu.VMEM((tm, D_MODEL), MXU_DT)],
        compiler_params=_params(("arbitrary", "arbitrary")),
        name="inproj",
    )(xs2, modrows, gain.reshape(1, D_MODEL), w_in_p)


LRU_CHUNK = 128
LRU_NB = 4
SCAN_SEGS = 8
LRU_HALO = 16


def _lru_kernel(xa_ref, ga_ref, cw_ref, cb_ref, w_ref, ba_ref, bx_ref, lam_ref,
                o_ref, u_s, hf_s, xe_s, as_s, bs_s, hs_s, ac_s, ho_s, e_s, *, n_ctx, n_lat):
    C = LRU_CHUNK
    W = LRU_BLOCK_W
    WB = LRU_NB * W
    SEG = C // SCAN_SEGS
    S = (n_ctx + n_lat) * C
    cw = cw_ref[...]
    cb = cb_ref[...]
    sub = lax.broadcasted_iota(jnp.int32, (SCAN_SEGS, W), 0)

    def conv_chunk(start, first, last):
        E = LRU_HALO
        x = xa_ref[0, pl.ds(start, C), :].astype(F32)
        prev = xa_ref[0, pl.ds(pl.multiple_of(jnp.maximum(start - E, 0), E), E), :].astype(F32)
        nxt = xa_ref[0, pl.ds(pl.multiple_of(jnp.minimum(start + C, S - E), E), E), :].astype(F32)
        xe_s[0:E, :] = jnp.where(first, 0.0, prev)
        xe_s[E:E + C, :] = x
        xe_s[E + C:, :] = jnp.where(last, 0.0, nxt)
        return (xe_s[E - 2:E - 2 + C, :] * cw[0:1] + xe_s[E - 1:E - 1 + C, :] * cw[1:2]
                + x * cw[2:3] + xe_s[E + 1:E + 1 + C, :] * cw[3:4] + cb)

    def coeffs(u, d):
        um = u.astype(MXU_DT)
        z = [jnp.dot(um[:, n * W:(n + 1) * W], w_ref[d, n], preferred_element_type=F32) for n in range(LRU_NB)]
        za = jnp.concatenate([zn[:, :W] for zn in z], axis=1)
        zx = jnp.concatenate([zn[:, W:] for zn in z], axis=1)
        r = jax.nn.sigmoid(za + ba_ref[d:d + 1, :])
        i = jax.nn.sigmoid(zx + bx_ref[d:d + 1, :])
        log_a = (-LRU_C * r) * _softplus(-lam_ref[d:d + 1, :])
        a = jnp.exp(log_a)
        g = -jnp.tanh(log_a) * (1.0 + a * a)
        root = jnp.where(g > 0.0, g * lax.rsqrt(g), 0.0)
        return a, root * (i * u)

    def scan_block(n, h, reverse):
        hk = ak = None
        for k in (range(SEG - 1, -1, -1) if reverse else range(SEG)):
            a_k = as_s[n, pl.ds(k, SCAN_SEGS, stride=SEG), :]
            b_k = bs_s[n, pl.ds(k, SCAN_SEGS, stride=SEG), :]
            if hk is None:
                hk, ak = b_k, a_k
            else:
                hk = a_k * hk + b_k
                ak = a_k * ak
            hs_s[n, k] = hk
            ac_s[n, k] = ak
        s = 1
        while s < SCAN_SEGS:
            shift = SCAN_SEGS - s if reverse else s
            m = sub < SCAN_SEGS - s if reverse else sub >= s
            h_o = pltpu.roll(hk, shift, 0)
            a_o = pltpu.roll(ak, shift, 0)
            hk = jnp.where(m, ak * h_o + hk, hk)
            ak = jnp.where(m, ak * a_o, ak)
            s *= 2
        end = hk + ak * h
        e_s[n] = end
        if reverse:
            cin = jnp.where(sub == SCAN_SEGS - 1, h, pltpu.roll(end, SCAN_SEGS - 1, 0))
            h_new = e_s[n, 0:1, :]
        else:
            cin = jnp.where(sub == 0, h, pltpu.roll(end, 1, 0))
            h_new = e_s[n, SCAN_SEGS - 1:SCAN_SEGS, :]
        for k in range(SEG):
            ho_s[n, pl.ds(k, SCAN_SEGS, stride=SEG), :] = hs_s[n, k] + ac_s[n, k] * cin
        return h_new

    def scan_chunk(a, b, h, reverse):
        for n in range(LRU_NB):
            as_s[n] = a[:, n * W:(n + 1) * W]
            bs_s[n] = b[:, n * W:(n + 1) * W]
        h_new = [scan_block(n, h[:, n * W:(n + 1) * W], reverse) for n in range(LRU_NB)]
        return (jnp.concatenate([ho_s[n] for n in range(LRU_NB)], axis=1),
                jnp.concatenate(h_new, axis=1))

    zero_h = jnp.zeros((1, WB), F32)

    def fwd_body(c, h):
        start = pl.multiple_of(c * C, C)
        first = jnp.logical_or(c == 0, c == n_ctx)
        last = jnp.logical_or(c == n_ctx - 1, c == n_ctx + n_lat - 1)
        u = conv_chunk(start, first, last)
        u_s[pl.ds(start, C), :] = u
        a, b = coeffs(u, 0)
        hfull, h = scan_chunk(a, b, h, False)
        hf_s[pl.ds(start, C), :] = hfull
        return h

    lax.fori_loop(0, n_ctx + n_lat, fwd_body, zero_h)

    def bwd_body(c, h):
        start = pl.multiple_of(c * C, C)
        u = u_s[pl.ds(start, C), :]
        a, b = coeffs(u, 1)
        hfull, h = scan_chunk(a, b, h, True)
        y = hf_s[pl.ds(start, C), :] + hfull
        o_ref[0, pl.ds(start, C), :] = (y * _silu(ga_ref[0, pl.ds(start, C), :].astype(F32))).astype(o_ref.dtype)
        return h

    h = lax.fori_loop(0, n_ctx, lambda i, h: bwd_body(n_ctx - 1 - i, h), zero_h)
    lax.fori_loop(0, n_lat, lambda i, h: bwd_body(n_ctx + n_lat - 1 - i, h), h)


def _lru_branch(p3, conv_w, conv_b, w_cat, ba, bx, lam, n_ctx_rows):
    B, S, _ = p3.shape
    C, W, NB = LRU_CHUNK, LRU_BLOCK_W, LRU_NB
    WB = NB * W
    n_ctx, n_lat = n_ctx_rows // C, (S - n_ctx_rows) // C
    xa0, ga0 = COL_XA // WB, COL_GA // WB
    per_block = lambda rows: pl.BlockSpec((rows, WB), lambda b, n: (0, n))
    return pl.pallas_call(
        functools.partial(_lru_kernel, n_ctx=n_ctx, n_lat=n_lat),
        grid=(B, LRU_BLOCKS // NB),
        in_specs=[pl.BlockSpec((1, S, WB), lambda b, n: (b, 0, xa0 + n)),
                  pl.BlockSpec((1, S, WB), lambda b, n: (b, 0, ga0 + n)),
                  per_block(CONV_W), per_block(1),
                  pl.BlockSpec((2, NB, W, 2 * W), lambda b, n: (0, n, 0, 0)),
                  per_block(2), per_block(2), per_block(2)],
        out_specs=pl.BlockSpec((1, S, WB), lambda b, n: (b, 0, n)),
        out_shape=jax.ShapeDtypeStruct((B, S, D_MODEL), MXU_DT),
        scratch_shapes=[pltpu.VMEM((S, WB), F32), pltpu.VMEM((S, WB), F32),
                        pltpu.VMEM((C + 2 * LRU_HALO, WB), F32), pltpu.VMEM((NB, C, W), F32), pltpu.VMEM((NB, C, W), F32),
                        pltpu.VMEM((NB, C // SCAN_SEGS, SCAN_SEGS, W), F32),
                        pltpu.VMEM((NB, C // SCAN_SEGS, SCAN_SEGS, W), F32),
                        pltpu.VMEM((NB, C, W), F32), pltpu.VMEM((NB, SCAN_SEGS, W), F32)],
        compiler_params=_params(("arbitrary", "arbitrary")),
        name="lru",
    )(p3, p3, conv_w, conv_b.reshape(1, D_MODEL), w_cat, ba, bx, lam)


def _swap_quarters(x, q):
    lane = lax.broadcasted_iota(jnp.int32, (x.shape[0], 128), 1)
    cols = []
    for c in range(x.shape[1] // 128):
        xc = x[:, c * 128:(c + 1) * 128]
        if 2 * q == 128:
            cols.append(pltpu.roll(xc, q, 1))
        else:
            cols.append(jnp.where(lane % (2 * q) < q, pltpu.roll(xc, 128 - q, 1), pltpu.roll(xc, q, 1)))
    return cols[0] if len(cols) == 1 else jnp.concatenate(cols, axis=1)


RET_SUB = 2


def _ret_block_index(d, c, n_ctx, n_all):
    rev = jnp.where(c < n_ctx, n_ctx - 1 - c, n_ctx + n_all - 1 - c)
    return jnp.where(d == 0, c, rev)


def _ret_kernel(q_ref, k_ref, v_ref, g_ref, cos_ref, sin_ref, theta_ref, gain_ref,
                o_ref, s_s, of_s, oc_s, inner_s, dec_s, *, n_ctx, n_all):
    C, dh = RET_CHUNK, RET_HEAD_DIM
    RB = RET_SUB * C
    d = pl.program_id(1)
    c = pl.program_id(2)
    start = pl.multiple_of(_ret_block_index(d, c, n_ctx, n_all) * RB, RB)
    fwd = d == 0
    wide = lambda t: jnp.concatenate([t] * (dh // C), axis=1)

    @pl.when(c == 0)
    def _():
        s_s[...] = jnp.zeros_like(s_s)
        row = lax.broadcasted_iota(jnp.int32, (C, C), 0)
        col = lax.broadcasted_iota(jnp.int32, (C, C), 1)
        diff = jnp.where(fwd, row - col, col - row).astype(F32)
        q_pow = jnp.where(fwd, row + 1, C - row).astype(F32)
        k_pow = jnp.where(fwd, C - 1 - row, row).astype(F32)
        for h in range(RET_HEADS):
            lg = -_softplus(-theta_ref[pl.ds(d * RET_HEADS + h, 1), :])
            inner_s[h] = jnp.where(diff >= 0, jnp.exp(lg * jnp.maximum(diff, 0.0)), 0.0)
            dec_s[h, 0] = jnp.exp(lg * q_pow)
            dec_s[h, 1] = jnp.exp(lg * k_pow)
            dec_s[h, 2] = jnp.broadcast_to(jnp.exp(lg * float(C)), (C, C))

    k_scale = dh ** -0.5
    for t in range(RET_SUB):
        r0 = pl.multiple_of(jnp.where(fwd, t, RET_SUB - 1 - t) * C, C)
        rows = pl.ds(r0, C)
        cos = cos_ref[rows, :]
        sin = sin_ref[rows, :]
        for h in range(RET_HEADS):
            cs = slice(h * dh, (h + 1) * dh)
            q = q_ref[0, rows, cs].astype(F32)
            k = k_ref[0, rows, cs].astype(F32)
            q = q * cos + _swap_quarters(q, dh // 4) * sin
            k = (k * cos + _swap_quarters(k, dh // 4) * sin) * k_scale
            qm = q.astype(MXU_DT)
            km = k.astype(MXU_DT)
            vm = v_ref[0, rows, cs].astype(MXU_DT)
            s_old = s_s[h]
            att = lax.dot_general(qm, km, (((1,), (1,)), ((), ())), preferred_element_type=F32) * inner_s[h]
            oc_s[rows, cs] = (jnp.dot(att.astype(MXU_DT), vm, preferred_element_type=F32)
                              + jnp.dot(qm, s_old.astype(MXU_DT), preferred_element_type=F32) * wide(dec_s[h, 0]))
            kt = jnp.transpose(k * wide(dec_s[h, 1])).astype(MXU_DT)
            s_s[h] = s_old * wide(dec_s[h, 2, 0:1, :]) + jnp.dot(kt, vm, preferred_element_type=F32)

    @pl.when(fwd)
    def _():
        of_s[pl.ds(start, RB), :] = oc_s[...]

    @pl.when(jnp.logical_not(fwd))
    def _():
        for h in range(RET_HEADS):
            cs = slice(h * dh, (h + 1) * dh)
            t = oc_s[:, cs] + of_s[pl.ds(start, RB), cs]
            tc = t - jnp.mean(t, axis=-1, keepdims=True)
            y = tc * lax.rsqrt(jnp.mean(tc * tc, axis=-1, keepdims=True) + NORM_EPS) * gain_ref[:, cs]
            o_ref[0, :, cs] = (y * _silu(g_ref[0, :, cs].astype(F32))).astype(o_ref.dtype)


def _ret_branch(p3, cos_t, sin_t, theta, gain, n_ctx_rows):
    B, S, _ = p3.shape
    C = RET_CHUNK
    RB = RET_SUB * C
    assert n_ctx_rows % RB == 0 and S % RB == 0
    n_ctx, n_all = n_ctx_rows // RB, S // RB
    bidx = functools.partial(_ret_block_index, n_ctx=n_ctx, n_all=n_all)

    def pcol(col):
        return pl.BlockSpec((1, RB, D_MODEL), lambda b, d, c: (b, bidx(d, c), col // D_MODEL))

    def out_map(b, d, c):
        return (b, jnp.where(d == 0, bidx(1, 0), bidx(d, c)), 0)

    return pl.pallas_call(
        functools.partial(_ret_kernel, n_ctx=n_ctx, n_all=n_all),
        grid=(B, 2, n_all),
        in_specs=[pcol(COL_QB), pcol(COL_KB), pcol(COL_VB), pcol(COL_GB),
                  pl.BlockSpec((RB, RET_HEAD_DIM), lambda b, d, c: (bidx(d, c), 0)),
                  pl.BlockSpec((RB, RET_HEAD_DIM), lambda b, d, c: (bidx(d, c), 0)),
                  pl.BlockSpec((2 * RET_HEADS, C), lambda b, d, c: (0, 0)),
                  pl.BlockSpec((1, D_MODEL), lambda b, d, c: (0, 0))],
        out_specs=pl.BlockSpec((1, RB, D_MODEL), out_map),
        out_shape=jax.ShapeDtypeStruct((B, S, D_MODEL), MXU_DT),
        scratch_shapes=[pltpu.VMEM((RET_HEADS, RET_HEAD_DIM, RET_HEAD_DIM), F32),
                        pltpu.VMEM((S, D_MODEL), F32), pltpu.VMEM((RB, D_MODEL), F32),
                        pltpu.VMEM((RET_HEADS, C, C), F32), pltpu.VMEM((RET_HEADS, 3, C, C), F32)],
        compiler_params=_params(("arbitrary", "arbitrary", "arbitrary")),
        name="retention",
    )(p3, p3, p3, p3, cos_t, sin_t, jnp.broadcast_to(theta.reshape(2 * RET_HEADS, 1), (2 * RET_HEADS, C)),
      gain.reshape(1, D_MODEL))


def _rms_rows(x, gain, width):
    return x * lax.rsqrt(jnp.sum(x * x, axis=-1, keepdims=True) * (1.0 / width) + NORM_EPS) * gain


def _mla_prep_kernel(p_ref, wq_ref, wkv_ref, qn_ref, kvn_ref, gqn_ref, gqr_ref, gkn_ref, gkr_ref,
                     cos_ref, sin_ref, q_ref, k_ref, v_ref):
    H = MLA_HEADS
    blk = p_ref[0].astype(F32)
    qd = blk[:, 0:Q_LORA]
    kvd = blk[:, Q_LORA:Q_LORA + KV_LORA]
    kr = blk[:, Q_LORA + KV_LORA:Q_LORA + KV_LORA + 128]
    tm = blk.shape[0]
    lane = lax.broadcasted_iota(jnp.int32, (tm, 128), 1)
    low = lane < MLA_ROPE
    cos = cos_ref[...]
    sin = sin_ref[...]

    q = jnp.dot(_rms_rows(qd, qn_ref[...], Q_LORA).astype(MXU_DT), wq_ref[...], preferred_element_type=F32)
    kv = jnp.dot(_rms_rows(kvd, kvn_ref[...], KV_LORA).astype(MXU_DT), wkv_ref[...], preferred_element_type=F32)

    k_rope = _rms_rows(kr, gkr_ref[...], MLA_ROPE)
    k_rope = k_rope * cos[:, 0:128] + _swap_quarters(k_rope, MLA_ROPE // 4) * sin[:, 0:128]
    k_rope = jnp.where(low, k_rope, 0.0)

    for j in range(H // 2):
        xr = q[:, H * MLA_NOPE + j * 128:H * MLA_NOPE + (j + 1) * 128]
        sq = xr * xr
        ms_lo = jnp.sum(jnp.where(low, sq, 0.0), axis=-1, keepdims=True)
        ms_hi = jnp.sum(jnp.where(low, 0.0, sq), axis=-1, keepdims=True)
        inv = lax.rsqrt(jnp.where(low, ms_lo, ms_hi) * (1.0 / MLA_ROPE) + NORM_EPS)
        xr = xr * inv * gqr_ref[...]
        xr = xr * cos[:, 0:128] + _swap_quarters(xr, MLA_ROPE // 4) * sin[:, 0:128]
        xr = xr * Q_SCALE
        halves = (jnp.where(low, xr, 0.0), jnp.where(low, pltpu.roll(xr, MLA_ROPE, 1), 0.0))
        for e in range(2):
            h = 2 * j + e
            qn = _rms_rows(q[:, h * MLA_NOPE:(h + 1) * MLA_NOPE], gqn_ref[...], MLA_NOPE) * Q_SCALE
            q_ref[0, h, :, 0:MLA_NOPE] = qn.astype(q_ref.dtype)
            q_ref[0, h, :, MLA_NOPE:] = halves[e].astype(q_ref.dtype)

    for h in range(H):
        kn = _rms_rows(kv[:, h * MLA_NOPE:(h + 1) * MLA_NOPE], gkn_ref[...], MLA_NOPE)
        k_ref[0, h, 0:MLA_NOPE, :] = jnp.transpose(kn).astype(k_ref.dtype)
        k_ref[0, h, MLA_NOPE:, :] = jnp.transpose(k_rope).astype(k_ref.dtype)
        v_ref[0, h, :, 0:MLA_V] = kv[:, H * MLA_NOPE + h * MLA_V:H * MLA_NOPE + (h + 1) * MLA_V].astype(v_ref.dtype)
        v_ref[0, h, :, MLA_V:] = jnp.where(lane == 0, 1.0, 0.0).astype(v_ref.dtype)


def _mla_prep(p3, wq, wkv, q_norm, kv_norm, g_qn, g_qr2, g_kn, g_kr2, cos_t, sin_t):
    B, S, _ = p3.shape
    H = MLA_HEADS
    tm = 256
    full = lambda shape: pl.BlockSpec(shape, lambda b, i: (0,) * len(shape))
    return pl.pallas_call(
        _mla_prep_kernel,
        grid=(B, S // tm),
        in_specs=[pl.BlockSpec((1, tm, 1024), lambda b, i: (b, i, COL_TAIL // 1024)),
                  full(wq.shape), full(wkv.shape),
                  full((1, Q_LORA)), full((1, KV_LORA)), full((1, MLA_NOPE)), full((1, 128)),
                  full((1, MLA_NOPE)), full((1, 128)),
                  pl.BlockSpec((tm, 128), lambda b, i: (i, 0)),
                  pl.BlockSpec((tm, 128), lambda b, i: (i, 0))],
        out_specs=[pl.BlockSpec((1, H, tm, MLA_QK_PAD), lambda b, i: (b, 0, i, 0)),
                   pl.BlockSpec((1, H, MLA_QK_PAD, tm), lambda b, i: (b, 0, 0, i)),
                   pl.BlockSpec((1, H, tm, MLA_V_PAD), lambda b, i: (b, 0, i, 0))],
        out_shape=[jax.ShapeDtypeStruct((B, H, S, MLA_QK_PAD), MXU_DT),
                   jax.ShapeDtypeStruct((B, H, MLA_QK_PAD, S), MXU_DT),
                   jax.ShapeDtypeStruct((B, H, S, MLA_V_PAD), MXU_DT)],
        compiler_params=_params(("arbitrary", "arbitrary")),
        name="mla_prep",
    )(p3, wq, wkv, q_norm.reshape(1, -1), kv_norm.reshape(1, -1), g_qn.reshape(1, -1), g_qr2,
      g_kn.reshape(1, -1), g_kr2, cos_t, sin_t)


ATT_TQ = 256
ATT_TK = 256
ATT_HP = 2


def _attn_kernel(q_ref, qn_ref, kt_ref, v_ref, g_ref, o_ref, s_s, m_s, *, n_first, n_all_tiles):
    tq, tk = ATT_TQ, ATT_TK
    i = pl.program_id(2)
    heads = range(ATT_HP)

    def scores(q, h, j):
        s = jnp.dot(q, kt_ref[0, h, :, j * tk:(j + 1) * tk], preferred_element_type=F32)
        s_s[h, :, j * tk:(j + 1) * tk] = s
        mm = s[:, 0:128]
        for c in range(1, tk // 128):
            mm = jnp.maximum(mm, s[:, c * 128:(c + 1) * 128])
        return mm

    def set_row_max(mm, h):
        m_s[h] = jnp.broadcast_to(jnp.max(mm, axis=-1, keepdims=True), (tq, 128))

    def stage(n_kv):
        for h in heads:
            q = q_ref[0, h]
            mm = scores(q, h, 0)
            for j in range(1, n_kv):
                mm = jnp.maximum(mm, scores(q, h, j))
            set_row_max(mm, h)

    def attend(n_cur, n_next):
        m = [m_s[h] for h in heads]
        qn = [qn_ref[0, h] for h in heads]
        acc = [jnp.zeros((tq, MLA_V_PAD), F32) for _ in heads]
        mm = [None for _ in heads]
        for j in range(max(n_cur, n_next)):
            for h in heads:
                if j < n_cur:
                    ps = [jnp.exp2(s_s[h, :, j * tk + c * 128:j * tk + (c + 1) * 128] - m[h]).astype(MXU_DT)
                          for c in range(tk // 128)]
                    acc[h] = acc[h] + jnp.dot(jnp.concatenate(ps, axis=1), v_ref[0, h, j * tk:(j + 1) * tk, :],
                                              preferred_element_type=F32)
                if j < n_next:
                    c = scores(qn[h], h, j)
                    mm[h] = c if mm[h] is None else jnp.maximum(mm[h], c)
        for h in heads:
            set_row_max(mm[h], h)
            l = jnp.sum(acc[h][:, MLA_V:], axis=-1, keepdims=True)
            cs = slice(h * MLA_V, (h + 1) * MLA_V)
            o_ref[0, :, cs] = (acc[h][:, :MLA_V] / l * _silu(g_ref[0, :, cs].astype(F32))).astype(o_ref.dtype)

    pl.when(i == 0)(lambda: stage(n_first))
    if n_first == n_all_tiles:
        attend(n_all_tiles, n_all_tiles)
    else:
        pl.when(i == 0)(lambda: attend(n_first, n_all_tiles))
        pl.when(i > 0)(lambda: attend(n_all_tiles, n_all_tiles))


def _mla_attention(qc, kt, vc, p3, n_ctx_rows, with_ctx):
    B, H, S, _ = qc.shape
    tq, HP = ATT_TQ, ATT_HP
    assert n_ctx_rows == tq and ATT_TK == tq and H % HP == 0
    n_all_tiles = S // ATT_TK
    q_off = 0 if with_ctx else 1
    n_q = S // tq - q_off
    n_first = n_ctx_rows // ATT_TK if with_ctx else n_all_tiles
    gc0 = COL_GC // (HP * MLA_V)
    return pl.pallas_call(
        functools.partial(_attn_kernel, n_first=n_first, n_all_tiles=n_all_tiles),
        grid=(B, H // HP, n_q),
        in_specs=[pl.BlockSpec((1, HP, tq, MLA_QK_PAD), lambda b, h, i: (b, h, i + q_off, 0)),
                  pl.BlockSpec((1, HP, tq, MLA_QK_PAD),
                               lambda b, h, i: (b, h, jnp.minimum(i + 1, n_q - 1) + q_off, 0)),
                  pl.BlockSpec((1, HP, MLA_QK_PAD, S), lambda b, h, i: (b, h, 0, 0)),
                  pl.BlockSpec((1, HP, S, MLA_V_PAD), lambda b, h, i: (b, h, 0, 0)),
                  pl.BlockSpec((1, tq, HP * MLA_V), lambda b, h, i: (b, i + q_off, gc0 + h))],
        out_specs=pl.BlockSpec((1, tq, HP * MLA_V), lambda b, h, i: (b, i + q_off, h)),
        out_shape=jax.ShapeDtypeStruct((B, S, H * MLA_V), MXU_DT),
        scratch_shapes=[pltpu.VMEM((HP, tq, S), F32), pltpu.VMEM((HP, tq, 128), F32)],
        compiler_params=_params(("arbitrary", "arbitrary", "arbitrary")),
        name="mla_attention",
    )(qc, qc, kt, vc, p3)


def _merge_kernel(ya_ref, yb_ref, yc_ref, ma_ref, mb_ref, mc_ref, x_ref, mod_ref,
                  wa_ref, wb_ref, wc_ref, wo_ref, o_ref):
    def down(y_ref, m_ref, w_ref):
        return jax.nn.sigmoid(m_ref[0].astype(F32)) * jnp.dot(y_ref[0], w_ref[...], preferred_element_type=F32)

    z = down(ya_ref, ma_ref, wa_ref) + down(yb_ref, mb_ref, wb_ref) + down(yc_ref, mc_ref, wc_ref)
    out = jnp.dot(z.astype(MXU_DT), wo_ref[...], preferred_element_type=F32)
    o_ref[0] = x_ref[0] + mod_ref[0, 2:3] * out


def _merge(ya, yb, yc, p3, xs, modrows, wa, wb, wc, wo, row_off):
    B, S, _ = xs.shape
    tm = SUBTILE
    o = row_off // tm
    n_sub = S // tm
    act = pl.BlockSpec((1, tm, D_MODEL), lambda b, i: (b, i + o, 0))
    gm = lambda k: pl.BlockSpec((1, tm, D_MODEL), lambda b, i: (b, i + o, COL_GM // D_MODEL + k))
    wspec = pl.BlockSpec((D_MODEL, D_MODEL), lambda b, i: (0, 0))
    return pl.pallas_call(
        _merge_kernel,
        grid=(B, n_sub - o),
        in_specs=[act, act, act, gm(0), gm(1), gm(2), act,
                  pl.BlockSpec((1, 8, D_MODEL), lambda b, i: (b * n_sub + i + o, 0, 0)),
                  wspec, wspec, wspec, wspec],
        out_specs=pl.BlockSpec((1, tm, D_MODEL), lambda b, i: (b, i, 0)),
        out_shape=jax.ShapeDtypeStruct((B, S - row_off, D_MODEL), F32),
        compiler_params=_params(("arbitrary", "arbitrary")),
        name="merge",
    )(ya, yb, yc, p3, p3, p3, xs, modrows, wa, wb, wc, wo)


def _rope_tables(n_ctx_rows, n_lat_rows, dim):
    rows = n_lat_rows // GRID_W
    row = jnp.repeat(jnp.arange(rows), GRID_W).astype(F32)
    col = jnp.tile(jnp.arange(GRID_W), rows).astype(F32)
    n_freq = dim // 4
    inv = ROPE_BASE ** (-jnp.arange(n_freq, dtype=F32) / n_freq)
    ang_r = row[:, None] * inv[None, :]
    ang_c = col[:, None] * inv[None, :]
    ang = jnp.concatenate([ang_r, ang_r, ang_c, ang_c], axis=-1)
    sign = jnp.tile(jnp.concatenate([-jnp.ones((n_freq,), F32), jnp.ones((n_freq,), F32)]), 2)
    cos = jnp.concatenate([jnp.ones((n_ctx_rows, dim), F32), jnp.cos(ang)], axis=0)
    sin = jnp.concatenate([jnp.zeros((n_ctx_rows, dim), F32), jnp.sin(ang) * sign], axis=0)
    return cos, sin


def _permute_w_in(w):
    xa_to_gb = w[:, :6144]
    tail = w[:, 6144:6848]
    gc_gm = w[:, 6848:]
    pad = jnp.zeros((w.shape[0], P_COLS - w.shape[1]), w.dtype)
    return jnp.concatenate([xa_to_gb, gc_gm, tail, pad], axis=1).astype(MXU_DT)


def _split_heads(w, first):
    K = w.shape[0]
    w3 = w.reshape(K, MLA_HEADS, -1)
    return jnp.concatenate([w3[:, :, :first].reshape(K, -1), w3[:, :, first:].reshape(K, -1)],
                           axis=1).astype(MXU_DT)


def _pad_lanes(g):
    return jnp.concatenate([g, jnp.zeros((128 - g.shape[0],), g.dtype)]).reshape(1, 128)


def kernel(x, c, ctx, c_ctx, w_mod, b_mod, norm_gain, w_in, conv_w, conv_b, lru_wa, lru_ba, lru_wx, lru_bx,
           lru_lambda, ret_theta, ret_gain, mla_q_norm, mla_w_q_up, mla_kv_norm, mla_w_kv_up,
           mla_g_qn, mla_g_qr, mla_g_kn, mla_g_kr, w_down_a, w_down_b, w_down_c, w_out):
    B, T, D = x.shape
    L = ctx.shape[1]
    S = L + T
    depth = w_in.shape[0]
    assert D == D_MODEL and L % SUBTILE == 0 and T % SUBTILE == 0 and T % GRID_W == 0

    cos_r, sin_r = _rope_tables(L, T, RET_HEAD_DIM)
    cos_m, sin_m = _rope_tables(L, T, MLA_ROPE)
    pad64 = lambda t: jnp.concatenate([t, t], axis=1)
    cos_m, sin_m = pad64(cos_m), pad64(sin_m)

    n_rows = ((B + 1 + 7) // 8) * 8
    cc = jnp.zeros((n_rows, D), F32).at[:B].set(c).at[B].set(c_ctx)
    n_sub = S // SUBTILE
    sub = jnp.arange(B * n_sub)
    mod_row = jnp.where(sub % n_sub < L // SUBTILE, B, sub // n_sub)

    xs = jnp.concatenate([ctx, x], axis=1)
    for l in range(depth):
        last = l == depth - 1
        mod = _modulation(cc, w_mod[l], b_mod[l])
        mod3 = mod.reshape(n_rows, 3, D)[mod_row]
        modrows = jnp.concatenate([mod3, jnp.zeros((B * n_sub, 5, D), F32)], axis=1)

        p3 = _inproj(xs.reshape(B * S, D), modrows, norm_gain[l], _permute_w_in(w_in[l])).reshape(B, S, P_COLS)

        w_cat = jnp.concatenate([lru_wa[l], lru_wx[l]], axis=-1).astype(MXU_DT)
        ya = _lru_branch(p3, conv_w[l], conv_b[l], w_cat, lru_ba[l], lru_bx[l], lru_lambda[l], L)
        yb = _ret_branch(p3, cos_r, sin_r, ret_theta[l], ret_gain[l], L)
        qc, kc, vc = _mla_prep(p3, _split_heads(mla_w_q_up[l], MLA_NOPE), _split_heads(mla_w_kv_up[l], MLA_NOPE),
                               mla_q_norm[l], mla_kv_norm[l], mla_g_qn[l],
                               jnp.concatenate([mla_g_qr[l], mla_g_qr[l]]).reshape(1, 128),
                               mla_g_kn[l], _pad_lanes(mla_g_kr[l]), cos_m, sin_m)
        yc = _mla_attention(qc, kc, vc, p3, L, not last)
        xs = _merge(ya, yb, yc, p3, xs, modrows,
                    w_down_a[l].astype(MXU_DT), w_down_b[l].astype(MXU_DT), w_down_c[l].astype(MXU_DT),
                    w_out[l].astype(MXU_DT), L if last else 0)
    return xs
```

```python
import functools
import math

import jax
import jax.numpy as jnp
import numpy as np
from jax import lax
from jax.experimental import pallas as pl
from jax.experimental.pallas import tpu as pltpu

D_MODEL = 1024
GRID_W = 64
ROPE_BASE = 10000.0
NORM_EPS = 1e-6

LRU_BLOCKS = 8
LRU_BLOCK_W = D_MODEL // LRU_BLOCKS
CONV_W = 4
LRU_C = 8.0

RET_HEADS = 4
RET_HEAD_DIM = D_MODEL // RET_HEADS
RET_CHUNK = 128

MLA_HEADS = 8
MLA_NOPE = 128
MLA_ROPE = 64
MLA_V = 128
Q_LORA = 384
KV_LORA = 256
MLA_SCALE = (MLA_NOPE + MLA_ROPE) ** -0.5
Q_SCALE = MLA_SCALE * math.log2(math.e)
MLA_V_PAD = 256
MLA_QK_PAD = 256

P_COLS = 11 * 1024
COL_XA, COL_GA, COL_QB, COL_KB, COL_VB, COL_GB, COL_GC, COL_GM, COL_TAIL = (
    0, 1024, 2048, 3072, 4096, 5120, 6144, 7168, 10240)

MXU_DT = jnp.bfloat16
P_DT = jnp.bfloat16
SUBTILE = 256
VMEM_LIMIT = 56 * 1024 * 1024
F32 = jnp.float32


def _params(sem):
    return pltpu.CompilerParams(dimension_semantics=sem, vmem_limit_bytes=VMEM_LIMIT)


def _silu(x):
    return x * jax.nn.sigmoid(x)


def _softplus(x):
    return jnp.maximum(x, 0.0) + jnp.log1p(jnp.exp(-jnp.abs(x)))


def _pick(n, cands):
    for c in cands:
        if n % c == 0:
            return c
    raise ValueError(f"no tile in {cands} divides {n}")


def _mod_kernel(c_ref, w_ref, b_ref, o_ref):
    a = _silu(c_ref[...])
    o_ref[...] = jnp.dot(a, w_ref[...], preferred_element_type=F32) + b_ref[...]


def _modulation(cc, w_mod, b_mod, layer):
    rows = cc.shape[0]
    depth, _, n3 = w_mod.shape
    tn = 1024
    return pl.pallas_call(
        _mod_kernel,
        grid=(n3 // tn,),
        in_specs=[pl.BlockSpec((rows, D_MODEL), lambda j: (0, 0)),
                  pl.BlockSpec((None, D_MODEL, tn), lambda j: (layer, 0, j)),
                  pl.BlockSpec((None, 1, tn), lambda j: (layer, 0, j))],
        out_specs=pl.BlockSpec((rows, tn), lambda j: (0, j)),
        out_shape=jax.ShapeDtypeStruct((rows, n3), F32),
        compiler_params=_params(("arbitrary",)),
        name="modulation",
    )(cc, w_mod, b_mod.reshape(depth, 1, n3))


def _inproj_kernel(x_ref, mod_ref, gain_ref, w_ref, o_ref, h_ref, *, tm):
    @pl.when(pl.program_id(1) == 0)
    def _():
        gain = gain_ref[...]
        for s in range(tm // SUBTILE):
            rows = slice(s * SUBTILE, (s + 1) * SUBTILE)
            x = x_ref[rows, :]
            y = x * lax.rsqrt(jnp.mean(x * x, axis=-1, keepdims=True) + NORM_EPS) * gain
            h_ref[rows, :] = (y * (1.0 + mod_ref[s, 1:2, :]) + mod_ref[s, 0:1, :]).astype(h_ref.dtype)

    o_ref[...] = jnp.dot(h_ref[...], w_ref[...], preferred_element_type=F32).astype(o_ref.dtype)


def _inproj(xs2, modrows, gain, w_in_p, layer):
    n = xs2.shape[0]
    tm = _pick(n, (1024, 512, 256))
    tn = P_COLS // 4
    return pl.pallas_call(
        functools.partial(_inproj_kernel, tm=tm),
        grid=(n // tm, P_COLS // tn),
        in_specs=[pl.BlockSpec((tm, D_MODEL), lambda i, j: (i, 0)),
                  pl.BlockSpec((tm // SUBTILE, 8, D_MODEL), lambda i, j: (i, 0, 0)),
                  pl.BlockSpec((1, D_MODEL), lambda i, j: (0, 0)),
                  pl.BlockSpec((None, D_MODEL, tn), lambda i, j: (layer, 0, j))],
        out_specs=pl.BlockSpec((tm, tn), lambda i, j: (i, j)),
        out_shape=jax.ShapeDtypeStruct((n, P_COLS), P_DT),
        scratch_shapes=[pltpu.VMEM((tm, D_MODEL), MXU_DT)],
        compiler_params=_params(("arbitrary", "arbitrary")),
        name="inproj",
    )(xs2, modrows, gain.reshape(1, D_MODEL), w_in_p)


LRU_CHUNK = 128
LRU_NB = 4
SCAN_SEGS = 8
LRU_HALO = 16
LRU_UNROLL = 2


def _lru_kernel(xa_ref, ga_ref, cw_ref, cb_ref, w_ref, ba_ref, bx_ref, lam_ref,
                o_ref, u_s, hf_s, xe_s, as_s, bs_s, hs_s, ac_s, ho_s, e_s, *, n_ctx, n_lat):
    C = LRU_CHUNK
    W = LRU_BLOCK_W
    WB = LRU_NB * W
    SEG = C // SCAN_SEGS
    S = (n_ctx + n_lat) * C
    cw = cw_ref[...]
    cb = cb_ref[...]
    sub = lax.broadcasted_iota(jnp.int32, (SCAN_SEGS, W), 0)

    def conv_chunk(start, first, last):
        E = LRU_HALO
        x = xa_ref[0, pl.ds(start, C), :].astype(F32)
        prev = xa_ref[0, pl.ds(pl.multiple_of(jnp.maximum(start - E, 0), E), E), :].astype(F32)
        nxt = xa_ref[0, pl.ds(pl.multiple_of(jnp.minimum(start + C, S - E), E), E), :].astype(F32)
        xe_s[0:E, :] = jnp.where(first, 0.0, prev)
        xe_s[E:E + C, :] = x
        xe_s[E + C:, :] = jnp.where(last, 0.0, nxt)
        return (xe_s[E - 2:E - 2 + C, :] * cw[0:1] + xe_s[E - 1:E - 1 + C, :] * cw[1:2]
                + x * cw[2:3] + xe_s[E + 1:E + 1 + C, :] * cw[3:4] + cb)

    def coeffs(u, d):
        um = u.astype(MXU_DT)
        z = [jnp.dot(um[:, n * W:(n + 1) * W], w_ref[d, n], preferred_element_type=F32) for n in range(LRU_NB)]
        za = jnp.concatenate([zn[:, :W] for zn in z], axis=1)
        zx = jnp.concatenate([zn[:, W:] for zn in z], axis=1)
        r = jax.nn.sigmoid(za + ba_ref[d:d + 1, :])
        i = jax.nn.sigmoid(zx + bx_ref[d:d + 1, :])
        log_a = (-LRU_C * r) * _softplus(-lam_ref[d:d + 1, :])
        a = jnp.exp(log_a)
        g = -jnp.tanh(log_a) * (1.0 + a * a)
        root = jnp.where(g > 0.0, g * lax.rsqrt(g), 0.0)
        return a, root * (i * u)

    def scan_block(n, h, reverse):
        hk = ak = None
        for k in (range(SEG - 1, -1, -1) if reverse else range(SEG)):
            a_k = as_s[n, pl.ds(k, SCAN_SEGS, stride=SEG), :]
            b_k = bs_s[n, pl.ds(k, SCAN_SEGS, stride=SEG), :]
            if hk is None:
                hk, ak = b_k, a_k
            else:
                hk = a_k * hk + b_k
                ak = a_k * ak
            hs_s[n, k] = hk
            ac_s[n, k] = ak
        s = 1
        while s < SCAN_SEGS:
            shift = SCAN_SEGS - s if reverse else s
            m = sub < SCAN_SEGS - s if reverse else sub >= s
            h_o = pltpu.roll(hk, shift, 0)
            a_o = pltpu.roll(ak, shift, 0)
            hk = jnp.where(m, ak * h_o + hk, hk)
            ak = jnp.where(m, ak * a_o, ak)
            s *= 2
        end = hk + ak * h
        e_s[n] = end
        if reverse:
            cin = jnp.where(sub == SCAN_SEGS - 1, h, pltpu.roll(end, SCAN_SEGS - 1, 0))
            h_new = e_s[n, 0:1, :]
        else:
            cin = jnp.where(sub == 0, h, pltpu.roll(end, 1, 0))
            h_new = e_s[n, SCAN_SEGS - 1:SCAN_SEGS, :]
        for k in range(SEG):
            ho_s[n, pl.ds(k, SCAN_SEGS, stride=SEG), :] = hs_s[n, k] + ac_s[n, k] * cin
        return h_new

    def scan_chunk(a, b, h, reverse):
        for n in range(LRU_NB):
            as_s[n] = a[:, n * W:(n + 1) * W]
            bs_s[n] = b[:, n * W:(n + 1) * W]
        h_new = [scan_block(n, h[:, n * W:(n + 1) * W], reverse) for n in range(LRU_NB)]
        return (jnp.concatenate([ho_s[n] for n in range(LRU_NB)], axis=1),
                jnp.concatenate(h_new, axis=1))

    zero_h = jnp.zeros((1, WB), F32)

    def fwd_body(c, h):
        start = pl.multiple_of(c * C, C)
        first = jnp.logical_or(c == 0, c == n_ctx)
        last = jnp.logical_or(c == n_ctx - 1, c == n_ctx + n_lat - 1)
        u = conv_chunk(start, first, last)
        u_s[pl.ds(start, C), :] = u
        a, b = coeffs(u, 0)
        hfull, h = scan_chunk(a, b, h, False)
        hf_s[pl.ds(start, C), :] = hfull
        return h

    lax.fori_loop(0, n_ctx + n_lat, fwd_body, zero_h, unroll=LRU_UNROLL)

    def bwd_body(c, h):
        start = pl.multiple_of(c * C, C)
        u = u_s[pl.ds(start, C), :]
        a, b = coeffs(u, 1)
        hfull, h = scan_chunk(a, b, h, True)
        y = hf_s[pl.ds(start, C), :] + hfull
        o_ref[0, pl.ds(start, C), :] = (y * _silu(ga_ref[0, pl.ds(start, C), :].astype(F32))).astype(o_ref.dtype)
        return h

    h = lax.fori_loop(0, n_ctx, lambda i, h: bwd_body(n_ctx - 1 - i, h), zero_h, unroll=LRU_UNROLL)
    lax.fori_loop(0, n_lat, lambda i, h: bwd_body(n_ctx + n_lat - 1 - i, h), h, unroll=LRU_UNROLL)


def _lru_branch(p3, conv_w, conv_b, w_cat, ba, bx, lam, n_ctx_rows):
    B, S, _ = p3.shape
    C, W, NB = LRU_CHUNK, LRU_BLOCK_W, LRU_NB
    WB = NB * W
    n_ctx, n_lat = n_ctx_rows // C, (S - n_ctx_rows) // C
    xa0, ga0 = COL_XA // WB, COL_GA // WB
    per_block = lambda rows: pl.BlockSpec((rows, WB), lambda b, n: (0, n))
    return pl.pallas_call(
        functools.partial(_lru_kernel, n_ctx=n_ctx, n_lat=n_lat),
        grid=(B, LRU_BLOCKS // NB),
        in_specs=[pl.BlockSpec((1, S, WB), lambda b, n: (b, 0, xa0 + n)),
                  pl.BlockSpec((1, S, WB), lambda b, n: (b, 0, ga0 + n)),
                  per_block(CONV_W), per_block(1),
                  pl.BlockSpec((2, NB, W, 2 * W), lambda b, n: (0, n, 0, 0)),
                  per_block(2), per_block(2), per_block(2)],
        out_specs=pl.BlockSpec((1, S, WB), lambda b, n: (b, 0, n)),
        out_shape=jax.ShapeDtypeStruct((B, S, D_MODEL), MXU_DT),
        scratch_shapes=[pltpu.VMEM((S, WB), F32), pltpu.VMEM((S, WB), F32),
                        pltpu.VMEM((C + 2 * LRU_HALO, WB), F32), pltpu.VMEM((NB, C, W), F32), pltpu.VMEM((NB, C, W), F32),
                        pltpu.VMEM((NB, C // SCAN_SEGS, SCAN_SEGS, W), F32),
                        pltpu.VMEM((NB, C // SCAN_SEGS, SCAN_SEGS, W), F32),
                        pltpu.VMEM((NB, C, W), F32), pltpu.VMEM((NB, SCAN_SEGS, W), F32)],
        compiler_params=_params(("arbitrary", "arbitrary")),
        name="lru",
    )(p3, p3, conv_w, conv_b.reshape(1, D_MODEL), w_cat, ba, bx, lam)


def _swap_quarters(x, q):
    lane = lax.broadcasted_iota(jnp.int32, (x.shape[0], 128), 1)
    cols = []
    for c in range(x.shape[1] // 128):
        xc = x[:, c * 128:(c + 1) * 128]
        if 2 * q == 128:
            cols.append(pltpu.roll(xc, q, 1))
        else:
            cols.append(jnp.where(lane % (2 * q) < q, pltpu.roll(xc, 128 - q, 1), pltpu.roll(xc, q, 1)))
    return cols[0] if len(cols) == 1 else jnp.concatenate(cols, axis=1)


RET_SUB = 2


def _ret_block_index(d, c, n_ctx, n_all):
    rev = jnp.where(c < n_ctx, n_ctx - 1 - c, n_ctx + n_all - 1 - c)
    return jnp.where(d == 0, c, rev)


def _ret_kernel(q_ref, k_ref, v_ref, g_ref, cos_ref, sin_ref, theta_ref, gain_ref,
                o_ref, s_s, of_s, oc_s, inner_s, dec_s, *, n_ctx, n_all):
    C, dh = RET_CHUNK, RET_HEAD_DIM
    RB = RET_SUB * C
    d = pl.program_id(1)
    c = pl.program_id(2)
    start = pl.multiple_of(_ret_block_index(d, c, n_ctx, n_all) * RB, RB)
    fwd = d == 0
    wide = lambda t: jnp.concatenate([t] * (dh // C), axis=1)

    @pl.when(c == 0)
    def _():
        s_s[...] = jnp.zeros_like(s_s)
        row = lax.broadcasted_iota(jnp.int32, (C, C), 0)
        col = lax.broadcasted_iota(jnp.int32, (C, C), 1)
        diff = jnp.where(fwd, row - col, col - row).astype(F32)
        q_pow = jnp.where(fwd, row + 1, C - row).astype(F32)
        k_pow = jnp.where(fwd, C - 1 - row, row).astype(F32)
        for h in range(RET_HEADS):
            lg = -_softplus(-theta_ref[pl.ds(d * RET_HEADS + h, 1), :])
            inner_s[h] = jnp.where(diff >= 0, jnp.exp(lg * jnp.maximum(diff, 0.0)), 0.0)
            dec_s[h, 0] = jnp.exp(lg * q_pow)
            dec_s[h, 1] = jnp.exp(lg * k_pow)
            dec_s[h, 2] = jnp.broadcast_to(jnp.exp(lg * float(C)), (C, C))

    k_scale = dh ** -0.5
    for t in range(RET_SUB):
        r0 = pl.multiple_of(jnp.where(fwd, t, RET_SUB - 1 - t) * C, C)
        rows = pl.ds(r0, C)
        cos = cos_ref[rows, :]
        sin = sin_ref[rows, :]
        for h in range(RET_HEADS):
            cs = slice(h * dh, (h + 1) * dh)
            q = q_ref[0, rows, cs].astype(F32)
            k = k_ref[0, rows, cs].astype(F32)
            q = q * cos + _swap_quarters(q, dh // 4) * sin
            k = (k * cos + _swap_quarters(k, dh // 4) * sin) * k_scale
            qm = q.astype(MXU_DT)
            km = k.astype(MXU_DT)
            vm = v_ref[0, rows, cs].astype(MXU_DT)
            s_old = s_s[h]
            att = lax.dot_general(qm, km, (((1,), (1,)), ((), ())), preferred_element_type=F32) * inner_s[h]
            oc_s[rows, cs] = (jnp.dot(att.astype(MXU_DT), vm, preferred_element_type=F32)
                              + jnp.dot(qm, s_old.astype(MXU_DT), preferred_element_type=F32) * wide(dec_s[h, 0]))
            kt = jnp.transpose(k * wide(dec_s[h, 1])).astype(MXU_DT)
            s_s[h] = s_old * wide(dec_s[h, 2, 0:1, :]) + jnp.dot(kt, vm, preferred_element_type=F32)

    @pl.when(fwd)
    def _():
        of_s[pl.ds(start, RB), :] = oc_s[...]

    @pl.when(jnp.logical_not(fwd))
    def _():
        for h in range(RET_HEADS):
            cs = slice(h * dh, (h + 1) * dh)
            t = oc_s[:, cs] + of_s[pl.ds(start, RB), cs]
            tc = t - jnp.mean(t, axis=-1, keepdims=True)
            y = tc * lax.rsqrt(jnp.mean(tc * tc, axis=-1, keepdims=True) + NORM_EPS) * gain_ref[:, cs]
            o_ref[0, :, cs] = (y * _silu(g_ref[0, :, cs].astype(F32))).astype(o_ref.dtype)


def _ret_branch(p3, cos_t, sin_t, theta, gain, n_ctx_rows):
    B, S, _ = p3.shape
    C = RET_CHUNK
    RB = RET_SUB * C
    assert n_ctx_rows % RB == 0 and S % RB == 0
    n_ctx, n_all = n_ctx_rows // RB, S // RB
    bidx = functools.partial(_ret_block_index, n_ctx=n_ctx, n_all=n_all)

    def pcol(col):
        return pl.BlockSpec((1, RB, D_MODEL), lambda b, d, c: (b, bidx(d, c), col // D_MODEL))

    def out_map(b, d, c):
        return (b, jnp.where(d == 0, bidx(1, 0), bidx(d, c)), 0)

    return pl.pallas_call(
        functools.partial(_ret_kernel, n_ctx=n_ctx, n_all=n_all),
        grid=(B, 2, n_all),
        in_specs=[pcol(COL_QB), pcol(COL_KB), pcol(COL_VB), pcol(COL_GB),
                  pl.BlockSpec((RB, RET_HEAD_DIM), lambda b, d, c: (bidx(d, c), 0)),
                  pl.BlockSpec((RB, RET_HEAD_DIM), lambda b, d, c: (bidx(d, c), 0)),
                  pl.BlockSpec((2 * RET_HEADS, C), lambda b, d, c: (0, 0)),
                  pl.BlockSpec((1, D_MODEL), lambda b, d, c: (0, 0))],
        out_specs=pl.BlockSpec((1, RB, D_MODEL), out_map),
        out_shape=jax.ShapeDtypeStruct((B, S, D_MODEL), MXU_DT),
        scratch_shapes=[pltpu.VMEM((RET_HEADS, RET_HEAD_DIM, RET_HEAD_DIM), F32),
                        pltpu.VMEM((S, D_MODEL), F32), pltpu.VMEM((RB, D_MODEL), F32),
                        pltpu.VMEM((RET_HEADS, C, C), F32), pltpu.VMEM((RET_HEADS, 3, C, C), F32)],
        compiler_params=_params(("arbitrary", "arbitrary", "arbitrary")),
        name="retention",
    )(p3, p3, p3, p3, cos_t, sin_t, jnp.broadcast_to(theta.reshape(2 * RET_HEADS, 1), (2 * RET_HEADS, C)),
      gain.reshape(1, D_MODEL))


def _rms_rows(x, gain, width):
    return x * lax.rsqrt(jnp.sum(x * x, axis=-1, keepdims=True) * (1.0 / width) + NORM_EPS) * gain


def _mla_prep_kernel(p_ref, wq_ref, wkv_ref, qn_ref, kvn_ref, gqn_ref, gqr_ref, gkn_ref, gkr_ref,
                     cos_ref, sin_ref, q_ref, k_ref, v_ref):
    H = MLA_HEADS
    blk = p_ref[0].astype(F32)
    qd = blk[:, 0:Q_LORA]
    kvd = blk[:, Q_LORA:Q_LORA + KV_LORA]
    kr = blk[:, Q_LORA + KV_LORA:Q_LORA + KV_LORA + 128]
    tm = blk.shape[0]
    lane = lax.broadcasted_iota(jnp.int32, (tm, 128), 1)
    low = lane < MLA_ROPE
    cos = cos_ref[...]
    sin = sin_ref[...]

    q = jnp.dot(_rms_rows(qd, qn_ref[...], Q_LORA).astype(MXU_DT), wq_ref[...], preferred_element_type=F32)
    kv = jnp.dot(_rms_rows(kvd, kvn_ref[...], KV_LORA).astype(MXU_DT), wkv_ref[...], preferred_element_type=F32)

    k_rope = _rms_rows(kr, gkr_ref[...], MLA_ROPE)
    k_rope = k_rope * cos[:, 0:128] + _swap_quarters(k_rope, MLA_ROPE // 4) * sin[:, 0:128]
    k_rope = jnp.where(low, k_rope, 0.0)

    for j in range(H // 2):
        xr = q[:, H * MLA_NOPE + j * 128:H * MLA_NOPE + (j + 1) * 128]
        sq = xr * xr
        ms_lo = jnp.sum(jnp.where(low, sq, 0.0), axis=-1, keepdims=True)
        ms_hi = jnp.sum(jnp.where(low, 0.0, sq), axis=-1, keepdims=True)
        inv = lax.rsqrt(jnp.where(low, ms_lo, ms_hi) * (1.0 / MLA_ROPE) + NORM_EPS)
        xr = xr * inv * gqr_ref[...]
        xr = xr * cos[:, 0:128] + _swap_quarters(xr, MLA_ROPE // 4) * sin[:, 0:128]
        xr = xr * Q_SCALE
        halves = (jnp.where(low, xr, 0.0), jnp.where(low, pltpu.roll(xr, MLA_ROPE, 1), 0.0))
        for e in range(2):
            h = 2 * j + e
            qn = _rms_rows(q[:, h * MLA_NOPE:(h + 1) * MLA_NOPE], gqn_ref[...], MLA_NOPE) * Q_SCALE
            q_ref[0, h, :, 0:MLA_NOPE] = qn.astype(q_ref.dtype)
            q_ref[0, h, :, MLA_NOPE:] = halves[e].astype(q_ref.dtype)

    for h in range(H):
        kn = _rms_rows(kv[:, h * MLA_NOPE:(h + 1) * MLA_NOPE], gkn_ref[...], MLA_NOPE)
        k_ref[0, h, 0:MLA_NOPE, :] = jnp.transpose(kn).astype(k_ref.dtype)
        k_ref[0, h, MLA_NOPE:, :] = jnp.transpose(k_rope).astype(k_ref.dtype)
        v_ref[0, h, :, 0:MLA_V] = kv[:, H * MLA_NOPE + h * MLA_V:H * MLA_NOPE + (h + 1) * MLA_V].astype(v_ref.dtype)
        v_ref[0, h, :, MLA_V:] = jnp.where(lane == 0, 1.0, 0.0).astype(v_ref.dtype)


def _mla_prep(p3, wq, wkv, q_norm, kv_norm, g_qn, g_qr2, g_kn, g_kr2, cos_t, sin_t):
    B, S, _ = p3.shape
    H = MLA_HEADS
    tm = 256
    full = lambda shape: pl.BlockSpec(shape, lambda b, i: (0,) * len(shape))
    return pl.pallas_call(
        _mla_prep_kernel,
        grid=(B, S // tm),
        in_specs=[pl.BlockSpec((1, tm, 1024), lambda b, i: (b, i, COL_TAIL // 1024)),
                  full(wq.shape), full(wkv.shape),
                  full((1, Q_LORA)), full((1, KV_LORA)), full((1, MLA_NOPE)), full((1, 128)),
                  full((1, MLA_NOPE)), full((1, 128)),
                  pl.BlockSpec((tm, 128), lambda b, i: (i, 0)),
                  pl.BlockSpec((tm, 128), lambda b, i: (i, 0))],
        out_specs=[pl.BlockSpec((1, H, tm, MLA_QK_PAD), lambda b, i: (b, 0, i, 0)),
                   pl.BlockSpec((1, H, MLA_QK_PAD, tm), lambda b, i: (b, 0, 0, i)),
                   pl.BlockSpec((1, H, tm, MLA_V_PAD), lambda b, i: (b, 0, i, 0))],
        out_shape=[jax.ShapeDtypeStruct((B, H, S, MLA_QK_PAD), MXU_DT),
                   jax.ShapeDtypeStruct((B, H, MLA_QK_PAD, S), MXU_DT),
                   jax.ShapeDtypeStruct((B, H, S, MLA_V_PAD), MXU_DT)],
        compiler_params=_params(("arbitrary", "arbitrary")),
        name="mla_prep",
    )(p3, wq, wkv, q_norm.reshape(1, -1), kv_norm.reshape(1, -1), g_qn.reshape(1, -1), g_qr2,
      g_kn.reshape(1, -1), g_kr2, cos_t, sin_t)


ATT_TQ = 256
ATT_TK = 256
ATT_HP = 2


def _attn_kernel(q_ref, qn_ref, kt_ref, v_ref, g_ref, o_ref, s_s, m_s, *, n_first, n_all_tiles):
    tq, tk = ATT_TQ, ATT_TK
    i = pl.program_id(2)
    heads = range(ATT_HP)

    def scores(q, h, j):
        s = jnp.dot(q, kt_ref[0, h, :, j * tk:(j + 1) * tk], preferred_element_type=F32)
        s_s[h, :, j * tk:(j + 1) * tk] = s
        mm = s[:, 0:128]
        for c in range(1, tk // 128):
            mm = jnp.maximum(mm, s[:, c * 128:(c + 1) * 128])
        return mm

    def set_row_max(mm, h):
        m_s[h] = jnp.broadcast_to(jnp.max(mm, axis=-1, keepdims=True), (tq, 128))

    def stage(n_kv):
        for h in heads:
            q = q_ref[0, h]
            mm = scores(q, h, 0)
            for j in range(1, n_kv):
                mm = jnp.maximum(mm, scores(q, h, j))
            set_row_max(mm, h)

    def attend(n_cur, n_next):
        m = [m_s[h] for h in heads]
        qn = [qn_ref[0, h] for h in heads]
        acc = [jnp.zeros((tq, MLA_V_PAD), F32) for _ in heads]
        mm = [None for _ in heads]
        for j in range(max(n_cur, n_next)):
            for h in heads:
                if j < n_cur:
                    ps = [jnp.exp2(s_s[h, :, j * tk + c * 128:j * tk + (c + 1) * 128] - m[h]).astype(MXU_DT)
                          for c in range(tk // 128)]
                    acc[h] = acc[h] + jnp.dot(jnp.concatenate(ps, axis=1), v_ref[0, h, j * tk:(j + 1) * tk, :],
                                              preferred_element_type=F32)
                if j < n_next:
                    c = scores(qn[h], h, j)
                    mm[h] = c if mm[h] is None else jnp.maximum(mm[h], c)
        for h in heads:
            set_row_max(mm[h], h)
            l = jnp.sum(acc[h][:, MLA_V:], axis=-1, keepdims=True)
            cs = slice(h * MLA_V, (h + 1) * MLA_V)
            o_ref[0, :, cs] = (acc[h][:, :MLA_V] / l * _silu(g_ref[0, :, cs].astype(F32))).astype(o_ref.dtype)

    pl.when(i == 0)(lambda: stage(n_first))
    if n_first == n_all_tiles:
        attend(n_all_tiles, n_all_tiles)
    else:
        pl.when(i == 0)(lambda: attend(n_first, n_all_tiles))
        pl.when(i > 0)(lambda: attend(n_all_tiles, n_all_tiles))


def _mla_attention(qc, kt, vc, p3, n_ctx_rows, with_ctx):
    B, H, S, _ = qc.shape
    tq, HP = ATT_TQ, ATT_HP
    assert n_ctx_rows == tq and ATT_TK == tq and H % HP == 0
    n_all_tiles = S // ATT_TK
    q_off = 0 if with_ctx else 1
    n_q = S // tq - q_off
    n_first = n_ctx_rows // ATT_TK if with_ctx else n_all_tiles
    gc0 = COL_GC // (HP * MLA_V)
    return pl.pallas_call(
        functools.partial(_attn_kernel, n_first=n_first, n_all_tiles=n_all_tiles),
        grid=(B, H // HP, n_q),
        in_specs=[pl.BlockSpec((1, HP, tq, MLA_QK_PAD), lambda b, h, i: (b, h, i + q_off, 0)),
                  pl.BlockSpec((1, HP, tq, MLA_QK_PAD),
                               lambda b, h, i: (b, h, jnp.minimum(i + 1, n_q - 1) + q_off, 0)),
                  pl.BlockSpec((1, HP, MLA_QK_PAD, S), lambda b, h, i: (b, h, 0, 0)),
                  pl.BlockSpec((1, HP, S, MLA_V_PAD), lambda b, h, i: (b, h, 0, 0)),
                  pl.BlockSpec((1, tq, HP * MLA_V), lambda b, h, i: (b, i + q_off, gc0 + h))],
        out_specs=pl.BlockSpec((1, tq, HP * MLA_V), lambda b, h, i: (b, i + q_off, h)),
        out_shape=jax.ShapeDtypeStruct((B, S, H * MLA_V), MXU_DT),
        scratch_shapes=[pltpu.VMEM((HP, tq, S), F32), pltpu.VMEM((HP, tq, 128), F32)],
        compiler_params=_params(("arbitrary", "arbitrary", "arbitrary")),
        name="mla_attention",
    )(qc, qc, kt, vc, p3)


def _merge_kernel(ya_ref, yb_ref, yc_ref, ma_ref, mb_ref, mc_ref, x_ref, mod_ref,
                  wa_ref, wb_ref, wc_ref, wo_ref, o_ref):
    def down(y_ref, m_ref, w_ref):
        return jax.nn.sigmoid(m_ref[0].astype(F32)) * jnp.dot(y_ref[0], w_ref[...], preferred_element_type=F32)

    z = down(ya_ref, ma_ref, wa_ref) + down(yb_ref, mb_ref, wb_ref) + down(yc_ref, mc_ref, wc_ref)
    out = jnp.dot(z.astype(MXU_DT), wo_ref[...], preferred_element_type=F32)
    o_ref[0] = x_ref[0] + mod_ref[0, 2:3] * out


def _merge(ya, yb, yc, p3, xs, modrows, wa, wb, wc, wo, row_off):
    B, S, _ = xs.shape
    tm = SUBTILE
    o = row_off // tm
    n_sub = S // tm
    act = pl.BlockSpec((1, tm, D_MODEL), lambda b, i: (b, i + o, 0))
    gm = lambda k: pl.BlockSpec((1, tm, D_MODEL), lambda b, i: (b, i + o, COL_GM // D_MODEL + k))
    wspec = pl.BlockSpec((D_MODEL, D_MODEL), lambda b, i: (0, 0))
    return pl.pallas_call(
        _merge_kernel,
        grid=(B, n_sub - o),
        in_specs=[act, act, act, gm(0), gm(1), gm(2), act,
                  pl.BlockSpec((1, 8, D_MODEL), lambda b, i: (b * n_sub + i + o, 0, 0)),
                  wspec, wspec, wspec, wspec],
        out_specs=pl.BlockSpec((1, tm, D_MODEL), lambda b, i: (b, i, 0)),
        out_shape=jax.ShapeDtypeStruct((B, S - row_off, D_MODEL), F32),
        compiler_params=_params(("arbitrary", "arbitrary")),
        name="merge",
    )(ya, yb, yc, p3, p3, p3, xs, modrows, wa, wb, wc, wo)


def _rope_tables(n_ctx_rows, n_lat_rows, dim):
    rows = n_lat_rows // GRID_W
    row = np.repeat(np.arange(rows), GRID_W).astype(np.float64)
    col = np.tile(np.arange(GRID_W), rows).astype(np.float64)
    n_freq = dim // 4
    inv = ROPE_BASE ** (-np.arange(n_freq, dtype=np.float64) / n_freq)
    ang_r = row[:, None] * inv[None, :]
    ang_c = col[:, None] * inv[None, :]
    ang = np.concatenate([ang_r, ang_r, ang_c, ang_c], axis=-1)
    sign = np.tile(np.concatenate([-np.ones(n_freq), np.ones(n_freq)]), 2)
    cos = np.concatenate([np.ones((n_ctx_rows, dim)), np.cos(ang)], axis=0)
    sin = np.concatenate([np.zeros((n_ctx_rows, dim)), np.sin(ang) * sign], axis=0)
    return cos.astype(np.float32), sin.astype(np.float32)


def _permute_w_in(w):
    xa_to_gb = w[..., :6144]
    tail = w[..., 6144:6848]
    gc_gm = w[..., 6848:]
    pad = jnp.zeros(w.shape[:-1] + (P_COLS - w.shape[-1],), w.dtype)
    return jnp.concatenate([xa_to_gb, gc_gm, tail, pad], axis=-1).astype(MXU_DT)


def _split_heads(w, first):
    K = w.shape[0]
    w3 = w.reshape(K, MLA_HEADS, -1)
    return jnp.concatenate([w3[:, :, :first].reshape(K, -1), w3[:, :, first:].reshape(K, -1)],
                           axis=1).astype(MXU_DT)


def _pad_lanes(g):
    return jnp.concatenate([g, jnp.zeros((128 - g.shape[0],), g.dtype)]).reshape(1, 128)


def kernel(x, c, ctx, c_ctx, w_mod, b_mod, norm_gain, w_in, conv_w, conv_b, lru_wa, lru_ba, lru_wx, lru_bx,
           lru_lambda, ret_theta, ret_gain, mla_q_norm, mla_w_q_up, mla_kv_norm, mla_w_kv_up,
           mla_g_qn, mla_g_qr, mla_g_kn, mla_g_kr, w_down_a, w_down_b, w_down_c, w_out):
    B, T, D = x.shape
    L = ctx.shape[1]
    S = L + T
    depth = w_in.shape[0]
    assert D == D_MODEL and L % SUBTILE == 0 and T % SUBTILE == 0 and T % GRID_W == 0

    cos_r, sin_r = _rope_tables(L, T, RET_HEAD_DIM)
    cos_m, sin_m = _rope_tables(L, T, MLA_ROPE)
    pad64 = lambda t: np.concatenate([t, t], axis=1)
    cos_m, sin_m = pad64(cos_m), pad64(sin_m)

    n_rows = ((B + 1 + 7) // 8) * 8
    cc = jnp.zeros((n_rows, D), F32).at[:B].set(c).at[B].set(c_ctx)
    n_sub = S // SUBTILE
    sub = jnp.arange(B * n_sub)
    mod_row = jnp.where(sub % n_sub < L // SUBTILE, B, sub // n_sub)

    xs = jnp.concatenate([ctx, x], axis=1)
    w_in_p = _permute_w_in(w_in)
    for l in range(depth):
        last = l == depth - 1
        mod = _modulation(cc, w_mod, b_mod, l)
        mod3 = mod.reshape(n_rows, 3, D)[mod_row]
        modrows = jnp.concatenate([mod3, jnp.zeros((B * n_sub, 5, D), F32)], axis=1)

        p3 = _inproj(xs.reshape(B * S, D), modrows, norm_gain[l], w_in_p, l).reshape(B, S, P_COLS)

        w_cat = jnp.concatenate([lru_wa[l], lru_wx[l]], axis=-1).astype(MXU_DT)
        ya = _lru_branch(p3, conv_w[l], conv_b[l], w_cat, lru_ba[l], lru_bx[l], lru_lambda[l], L)
        yb = _ret_branch(p3, cos_r, sin_r, ret_theta[l], ret_gain[l], L)
        qc, kc, vc = _mla_prep(p3, _split_heads(mla_w_q_up[l], MLA_NOPE), _split_heads(mla_w_kv_up[l], MLA_NOPE),
                               mla_q_norm[l], mla_kv_norm[l], mla_g_qn[l],
                               jnp.concatenate([mla_g_qr[l], mla_g_qr[l]]).reshape(1, 128),
                               mla_g_kn[l], _pad_lanes(mla_g_kr[l]), cos_m, sin_m)
        yc = _mla_attention(qc, kc, vc, p3, L, not last)
        xs = _merge(ya, yb, yc, p3, xs, modrows,
                    w_down_a[l].astype(MXU_DT), w_down_b[l].astype(MXU_DT), w_down_c[l].astype(MXU_DT),
                    w_out[l].astype(MXU_DT), L if last else 0)
    return xs
```

```python
import functools
import math

import jax
import jax.numpy as jnp
import numpy as np
from jax import lax
from jax.experimental import pallas as pl
from jax.experimental.pallas import tpu as pltpu

D_MODEL = 1024
GRID_W = 64
ROPE_BASE = 10000.0
NORM_EPS = 1e-6

LRU_BLOCKS = 8
LRU_BLOCK_W = D_MODEL // LRU_BLOCKS
CONV_W = 4
LRU_C = 8.0

RET_HEADS = 4
RET_HEAD_DIM = D_MODEL // RET_HEADS
RET_CHUNK = 128

MLA_HEADS = 8
MLA_NOPE = 128
MLA_ROPE = 64
MLA_V = 128
Q_LORA = 384
KV_LORA = 256
MLA_SCALE = (MLA_NOPE + MLA_ROPE) ** -0.5
Q_SCALE = MLA_SCALE * math.log2(math.e)
MLA_VT_PAD = 144
MLA_QK_PAD = 256

P_COLS = 11 * 1024
COL_XA, COL_GA, COL_QB, COL_KB, COL_VB, COL_GB, COL_GC, COL_GM, COL_TAIL = (
    0, 1024, 2048, 3072, 4096, 5120, 6144, 7168, 10240)

MXU_DT = jnp.bfloat16
P_DT = jnp.bfloat16
SUBTILE = 256
VMEM_LIMIT = 56 * 1024 * 1024
F32 = jnp.float32


def _params(sem):
    return pltpu.CompilerParams(dimension_semantics=sem, vmem_limit_bytes=VMEM_LIMIT)


def _silu(x):
    return x * jax.nn.sigmoid(x)


def _softplus(x):
    return jnp.maximum(x, 0.0) + jnp.log1p(jnp.exp(-jnp.abs(x)))


def _pick(n, cands):
    for c in cands:
        if n % c == 0:
            return c
    raise ValueError(f"no tile in {cands} divides {n}")


def _mod_kernel(c_ref, w_ref, b_ref, o_ref):
    a = _silu(c_ref[...])
    o_ref[...] = jnp.dot(a, w_ref[...], preferred_element_type=F32) + b_ref[...]


def _modulation(cc, w_mod, b_mod, layer):
    rows = cc.shape[0]
    depth, _, n3 = w_mod.shape
    tn = 1024
    return pl.pallas_call(
        _mod_kernel,
        grid=(n3 // tn,),
        in_specs=[pl.BlockSpec((rows, D_MODEL), lambda j: (0, 0)),
                  pl.BlockSpec((None, D_MODEL, tn), lambda j: (layer, 0, j)),
                  pl.BlockSpec((None, 1, tn), lambda j: (layer, 0, j))],
        out_specs=pl.BlockSpec((rows, tn), lambda j: (0, j)),
        out_shape=jax.ShapeDtypeStruct((rows, n3), F32),
        compiler_params=_params(("arbitrary",)),
        name="modulation",
    )(cc, w_mod, b_mod.reshape(depth, 1, n3))


def _inproj_kernel(x_ref, mod_ref, gain_ref, w_ref, o_ref, h_ref, *, tm):
    @pl.when(pl.program_id(1) == 0)
    def _():
        gain = gain_ref[...]
        for s in range(tm // SUBTILE):
            rows = slice(s * SUBTILE, (s + 1) * SUBTILE)
            x = x_ref[rows, :]
            y = x * lax.rsqrt(jnp.mean(x * x, axis=-1, keepdims=True) + NORM_EPS) * gain
            h_ref[rows, :] = (y * (1.0 + mod_ref[s, 1:2, :]) + mod_ref[s, 0:1, :]).astype(h_ref.dtype)

    o_ref[...] = jnp.dot(h_ref[...], w_ref[...], preferred_element_type=F32).astype(o_ref.dtype)


def _inproj(xs2, modrows, gain, w_in_p, layer):
    n = xs2.shape[0]
    tm = _pick(n, (1024, 512, 256))
    tn = P_COLS // 4
    return pl.pallas_call(
        functools.partial(_inproj_kernel, tm=tm),
        grid=(n // tm, P_COLS // tn),
        in_specs=[pl.BlockSpec((tm, D_MODEL), lambda i, j: (i, 0)),
                  pl.BlockSpec((tm // SUBTILE, 8, D_MODEL), lambda i, j: (i, 0, 0)),
                  pl.BlockSpec((1, D_MODEL), lambda i, j: (0, 0)),
                  pl.BlockSpec((None, D_MODEL, tn), lambda i, j: (layer, 0, j))],
        out_specs=pl.BlockSpec((tm, tn), lambda i, j: (i, j)),
        out_shape=jax.ShapeDtypeStruct((n, P_COLS), P_DT),
        scratch_shapes=[pltpu.VMEM((tm, D_MODEL), MXU_DT)],
        compiler_params=_params(("arbitrary", "arbitrary")),
        name="inproj",
    )(xs2, modrows, gain.reshape(1, D_MODEL), w_in_p)


LRU_CHUNK = 128
LRU_NB = 4
SCAN_SEGS = 8
LRU_HALO = 16
LRU_UNROLL = 2


def _lru_kernel(xa_ref, ga_ref, cw_ref, cb_ref, w_ref, ba_ref, bx_ref, lam_ref,
                o_ref, u_s, hf_s, xe_s, as_s, bs_s, hs_s, ac_s, ho_s, e_s, *, n_ctx, n_lat):
    C = LRU_CHUNK
    W = LRU_BLOCK_W
    WB = LRU_NB * W
    SEG = C // SCAN_SEGS
    S = (n_ctx + n_lat) * C
    cw = cw_ref[...]
    cb = cb_ref[...]
    sub = lax.broadcasted_iota(jnp.int32, (SCAN_SEGS, W), 0)

    def conv_chunk(start, first, last):
        E = LRU_HALO
        x = xa_ref[0, pl.ds(start, C), :].astype(F32)
        prev = xa_ref[0, pl.ds(pl.multiple_of(jnp.maximum(start - E, 0), E), E), :].astype(F32)
        nxt = xa_ref[0, pl.ds(pl.multiple_of(jnp.minimum(start + C, S - E), E), E), :].astype(F32)
        xe_s[0:E, :] = jnp.where(first, 0.0, prev)
        xe_s[E:E + C, :] = x
        xe_s[E + C:, :] = jnp.where(last, 0.0, nxt)
        return (xe_s[E - 2:E - 2 + C, :] * cw[0:1] + xe_s[E - 1:E - 1 + C, :] * cw[1:2]
                + x * cw[2:3] + xe_s[E + 1:E + 1 + C, :] * cw[3:4] + cb)

    def coeffs(u, d):
        um = u.astype(MXU_DT)
        z = [jnp.dot(um[:, n * W:(n + 1) * W], w_ref[d, n], preferred_element_type=F32) for n in range(LRU_NB)]
        za = jnp.concatenate([zn[:, :W] for zn in z], axis=1)
        zx = jnp.concatenate([zn[:, W:] for zn in z], axis=1)
        r = jax.nn.sigmoid(za + ba_ref[d:d + 1, :])
        i = jax.nn.sigmoid(zx + bx_ref[d:d + 1, :])
        log_a = (-LRU_C * r) * _softplus(-lam_ref[d:d + 1, :])
        a = jnp.exp(log_a)
        g = -jnp.tanh(log_a) * (1.0 + a * a)
        root = jnp.where(g > 0.0, g * lax.rsqrt(g), 0.0)
        return a, root * (i * u)

    def scan_block(n, h, reverse):
        hk = ak = None
        for k in (range(SEG - 1, -1, -1) if reverse else range(SEG)):
            a_k = as_s[n, pl.ds(k, SCAN_SEGS, stride=SEG), :]
            b_k = bs_s[n, pl.ds(k, SCAN_SEGS, stride=SEG), :]
            if hk is None:
                hk, ak = b_k, a_k
            else:
                hk = a_k * hk + b_k
                ak = a_k * ak
            hs_s[n, k] = hk
            ac_s[n, k] = ak
        s = 1
        while s < SCAN_SEGS:
            shift = SCAN_SEGS - s if reverse else s
            m = sub < SCAN_SEGS - s if reverse else sub >= s
            h_o = pltpu.roll(hk, shift, 0)
            a_o = pltpu.roll(ak, shift, 0)
            hk = jnp.where(m, ak * h_o + hk, hk)
            ak = jnp.where(m, ak * a_o, ak)
            s *= 2
        end = hk + ak * h
        e_s[n] = end
        if reverse:
            cin = jnp.where(sub == SCAN_SEGS - 1, h, pltpu.roll(end, SCAN_SEGS - 1, 0))
            h_new = e_s[n, 0:1, :]
        else:
            cin = jnp.where(sub == 0, h, pltpu.roll(end, 1, 0))
            h_new = e_s[n, SCAN_SEGS - 1:SCAN_SEGS, :]
        for k in range(SEG):
            ho_s[n, pl.ds(k, SCAN_SEGS, stride=SEG), :] = hs_s[n, k] + ac_s[n, k] * cin
        return h_new

    def scan_chunk(a, b, h, reverse):
        for n in range(LRU_NB):
            as_s[n] = a[:, n * W:(n + 1) * W]
            bs_s[n] = b[:, n * W:(n + 1) * W]
        h_new = [scan_block(n, h[:, n * W:(n + 1) * W], reverse) for n in range(LRU_NB)]
        return (jnp.concatenate([ho_s[n] for n in range(LRU_NB)], axis=1),
                jnp.concatenate(h_new, axis=1))

    zero_h = jnp.zeros((1, WB), F32)

    def fwd_body(c, h):
        start = pl.multiple_of(c * C, C)
        first = jnp.logical_or(c == 0, c == n_ctx)
        last = jnp.logical_or(c == n_ctx - 1, c == n_ctx + n_lat - 1)
        u = conv_chunk(start, first, last)
        u_s[pl.ds(start, C), :] = u
        a, b = coeffs(u, 0)
        hfull, h = scan_chunk(a, b, h, False)
        hf_s[pl.ds(start, C), :] = hfull
        return h

    lax.fori_loop(0, n_ctx + n_lat, fwd_body, zero_h, unroll=LRU_UNROLL)

    def bwd_body(c, h):
        start = pl.multiple_of(c * C, C)
        u = u_s[pl.ds(start, C), :]
        a, b = coeffs(u, 1)
        hfull, h = scan_chunk(a, b, h, True)
        y = hf_s[pl.ds(start, C), :] + hfull
        o_ref[0, pl.ds(start, C), :] = (y * _silu(ga_ref[0, pl.ds(start, C), :].astype(F32))).astype(o_ref.dtype)
        return h

    h = lax.fori_loop(0, n_ctx, lambda i, h: bwd_body(n_ctx - 1 - i, h), zero_h, unroll=LRU_UNROLL)
    lax.fori_loop(0, n_lat, lambda i, h: bwd_body(n_ctx + n_lat - 1 - i, h), h, unroll=LRU_UNROLL)


def _lru_branch(p3, conv_w, conv_b, w_cat, ba, bx, lam, n_ctx_rows):
    B, S, _ = p3.shape
    C, W, NB = LRU_CHUNK, LRU_BLOCK_W, LRU_NB
    WB = NB * W
    n_ctx, n_lat = n_ctx_rows // C, (S - n_ctx_rows) // C
    xa0, ga0 = COL_XA // WB, COL_GA // WB
    per_block = lambda rows: pl.BlockSpec((rows, WB), lambda b, n: (0, n))
    return pl.pallas_call(
        functools.partial(_lru_kernel, n_ctx=n_ctx, n_lat=n_lat),
        grid=(B, LRU_BLOCKS // NB),
        in_specs=[pl.BlockSpec((1, S, WB), lambda b, n: (b, 0, xa0 + n)),
                  pl.BlockSpec((1, S, WB), lambda b, n: (b, 0, ga0 + n)),
                  per_block(CONV_W), per_block(1),
                  pl.BlockSpec((2, NB, W, 2 * W), lambda b, n: (0, n, 0, 0)),
                  per_block(2), per_block(2), per_block(2)],
        out_specs=pl.BlockSpec((1, S, WB), lambda b, n: (b, 0, n)),
        out_shape=jax.ShapeDtypeStruct((B, S, D_MODEL), MXU_DT),
        scratch_shapes=[pltpu.VMEM((S, WB), F32), pltpu.VMEM((S, WB), F32),
                        pltpu.VMEM((C + 2 * LRU_HALO, WB), F32), pltpu.VMEM((NB, C, W), F32), pltpu.VMEM((NB, C, W), F32),
                        pltpu.VMEM((NB, C // SCAN_SEGS, SCAN_SEGS, W), F32),
                        pltpu.VMEM((NB, C // SCAN_SEGS, SCAN_SEGS, W), F32),
                        pltpu.VMEM((NB, C, W), F32), pltpu.VMEM((NB, SCAN_SEGS, W), F32)],
        compiler_params=_params(("arbitrary", "arbitrary")),
        name="lru",
    )(p3, p3, conv_w, conv_b.reshape(1, D_MODEL), w_cat, ba, bx, lam)


def _swap_quarters(x, q):
    lane = lax.broadcasted_iota(jnp.int32, (x.shape[0], 128), 1)
    cols = []
    for c in range(x.shape[1] // 128):
        xc = x[:, c * 128:(c + 1) * 128]
        if 2 * q == 128:
            cols.append(pltpu.roll(xc, q, 1))
        else:
            cols.append(jnp.where(lane % (2 * q) < q, pltpu.roll(xc, 128 - q, 1), pltpu.roll(xc, q, 1)))
    return cols[0] if len(cols) == 1 else jnp.concatenate(cols, axis=1)


RET_SUB = 2


def _ret_block_index(d, c, n_ctx, n_all):
    rev = jnp.where(c < n_ctx, n_ctx - 1 - c, n_ctx + n_all - 1 - c)
    return jnp.where(d == 0, c, rev)


def _ret_kernel(q_ref, k_ref, v_ref, g_ref, cos_ref, sin_ref, theta_ref, gain_ref,
                o_ref, s_s, of_s, oc_s, inner_s, dec_s, *, n_ctx, n_all):
    C, dh = RET_CHUNK, RET_HEAD_DIM
    RB = RET_SUB * C
    d = pl.program_id(1)
    c = pl.program_id(2)
    start = pl.multiple_of(_ret_block_index(d, c, n_ctx, n_all) * RB, RB)
    fwd = d == 0
    wide = lambda t: jnp.concatenate([t] * (dh // C), axis=1)

    @pl.when(c == 0)
    def _():
        s_s[...] = jnp.zeros_like(s_s)
        row = lax.broadcasted_iota(jnp.int32, (C, C), 0)
        col = lax.broadcasted_iota(jnp.int32, (C, C), 1)
        diff = jnp.where(fwd, row - col, col - row).astype(F32)
        q_pow = jnp.where(fwd, row + 1, C - row).astype(F32)
        k_pow = jnp.where(fwd, C - 1 - row, row).astype(F32)
        for h in range(RET_HEADS):
            lg = -_softplus(-theta_ref[pl.ds(d * RET_HEADS + h, 1), :])
            inner_s[h] = jnp.where(diff >= 0, jnp.exp(lg * jnp.maximum(diff, 0.0)), 0.0)
            dec_s[h, 0] = jnp.exp(lg * q_pow)
            dec_s[h, 1] = jnp.exp(lg * k_pow)
            dec_s[h, 2] = jnp.broadcast_to(jnp.exp(lg * float(C)), (C, C))

    k_scale = dh ** -0.5
    for t in range(RET_SUB):
        r0 = pl.multiple_of(jnp.where(fwd, t, RET_SUB - 1 - t) * C, C)
        rows = pl.ds(r0, C)
        cos = cos_ref[rows, :]
        sin = sin_ref[rows, :]
        for h in range(RET_HEADS):
            cs = slice(h * dh, (h + 1) * dh)
            q = q_ref[0, rows, cs].astype(F32)
            k = k_ref[0, rows, cs].astype(F32)
            q = q * cos + _swap_quarters(q, dh // 4) * sin
            k = (k * cos + _swap_quarters(k, dh // 4) * sin) * k_scale
            qm = q.astype(MXU_DT)
            km = k.astype(MXU_DT)
            vm = v_ref[0, rows, cs].astype(MXU_DT)
            s_old = s_s[h]
            att = lax.dot_general(qm, km, (((1,), (1,)), ((), ())), preferred_element_type=F32) * inner_s[h]
            oc_s[rows, cs] = (jnp.dot(att.astype(MXU_DT), vm, preferred_element_type=F32)
                              + jnp.dot(qm, s_old.astype(MXU_DT), preferred_element_type=F32) * wide(dec_s[h, 0]))
            kt = jnp.transpose(k * wide(dec_s[h, 1])).astype(MXU_DT)
            s_s[h] = s_old * wide(dec_s[h, 2, 0:1, :]) + jnp.dot(kt, vm, preferred_element_type=F32)

    @pl.when(fwd)
    def _():
        of_s[pl.ds(start, RB), :] = oc_s[...]

    @pl.when(jnp.logical_not(fwd))
    def _():
        for h in range(RET_HEADS):
            cs = slice(h * dh, (h + 1) * dh)
            t = oc_s[:, cs] + of_s[pl.ds(start, RB), cs]
            tc = t - jnp.mean(t, axis=-1, keepdims=True)
            y = tc * lax.rsqrt(jnp.mean(tc * tc, axis=-1, keepdims=True) + NORM_EPS) * gain_ref[:, cs]
            o_ref[0, :, cs] = (y * _silu(g_ref[0, :, cs].astype(F32))).astype(o_ref.dtype)


def _ret_branch(p3, cos_t, sin_t, theta, gain, n_ctx_rows):
    B, S, _ = p3.shape
    C = RET_CHUNK
    RB = RET_SUB * C
    assert n_ctx_rows % RB == 0 and S % RB == 0
    n_ctx, n_all = n_ctx_rows // RB, S // RB
    bidx = functools.partial(_ret_block_index, n_ctx=n_ctx, n_all=n_all)

    def pcol(col):
        return pl.BlockSpec((1, RB, D_MODEL), lambda b, d, c: (b, bidx(d, c), col // D_MODEL))

    def out_map(b, d, c):
        return (b, jnp.where(d == 0, bidx(1, 0), bidx(d, c)), 0)

    return pl.pallas_call(
        functools.partial(_ret_kernel, n_ctx=n_ctx, n_all=n_all),
        grid=(B, 2, n_all),
        in_specs=[pcol(COL_QB), pcol(COL_KB), pcol(COL_VB), pcol(COL_GB),
                  pl.BlockSpec((RB, RET_HEAD_DIM), lambda b, d, c: (bidx(d, c), 0)),
                  pl.BlockSpec((RB, RET_HEAD_DIM), lambda b, d, c: (bidx(d, c), 0)),
                  pl.BlockSpec((2 * RET_HEADS, C), lambda b, d, c: (0, 0)),
                  pl.BlockSpec((1, D_MODEL), lambda b, d, c: (0, 0))],
        out_specs=pl.BlockSpec((1, RB, D_MODEL), out_map),
        out_shape=jax.ShapeDtypeStruct((B, S, D_MODEL), MXU_DT),
        scratch_shapes=[pltpu.VMEM((RET_HEADS, RET_HEAD_DIM, RET_HEAD_DIM), F32),
                        pltpu.VMEM((S, D_MODEL), F32), pltpu.VMEM((RB, D_MODEL), F32),
                        pltpu.VMEM((RET_HEADS, C, C), F32), pltpu.VMEM((RET_HEADS, 3, C, C), F32)],
        compiler_params=_params(("arbitrary", "arbitrary", "arbitrary")),
        name="retention",
    )(p3, p3, p3, p3, cos_t, sin_t, jnp.broadcast_to(theta.reshape(2 * RET_HEADS, 1), (2 * RET_HEADS, C)),
      gain.reshape(1, D_MODEL))


def _rms_rows(x, gain, width):
    return x * lax.rsqrt(jnp.sum(x * x, axis=-1, keepdims=True) * (1.0 / width) + NORM_EPS) * gain


def _mla_prep_kernel(p_ref, wq_ref, wkv_ref, qn_ref, kvn_ref, gqn_ref, gqr_ref, gkn_ref, gkr_ref,
                     cos_ref, sin_ref, q_ref, k_ref, v_ref):
    H = MLA_HEADS
    blk = p_ref[0].astype(F32)
    qd = blk[:, 0:Q_LORA]
    kvd = blk[:, Q_LORA:Q_LORA + KV_LORA]
    kr = blk[:, Q_LORA + KV_LORA:Q_LORA + KV_LORA + 128]
    tm = blk.shape[0]
    lane = lax.broadcasted_iota(jnp.int32, (tm, 128), 1)
    low = lane < MLA_ROPE
    cos = cos_ref[...]
    sin = sin_ref[...]

    q = jnp.dot(_rms_rows(qd, qn_ref[...], Q_LORA).astype(MXU_DT), wq_ref[...], preferred_element_type=F32)
    kv = jnp.dot(_rms_rows(kvd, kvn_ref[...], KV_LORA).astype(MXU_DT), wkv_ref[...], preferred_element_type=F32)

    k_rope = _rms_rows(kr, gkr_ref[...], MLA_ROPE)
    k_rope = k_rope * cos[:, 0:128] + _swap_quarters(k_rope, MLA_ROPE // 4) * sin[:, 0:128]
    k_rope = jnp.where(low, k_rope, 0.0)

    for j in range(H // 2):
        xr = q[:, H * MLA_NOPE + j * 128:H * MLA_NOPE + (j + 1) * 128]
        sq = xr * xr
        ms_lo = jnp.sum(jnp.where(low, sq, 0.0), axis=-1, keepdims=True)
        ms_hi = jnp.sum(jnp.where(low, 0.0, sq), axis=-1, keepdims=True)
        inv = lax.rsqrt(jnp.where(low, ms_lo, ms_hi) * (1.0 / MLA_ROPE) + NORM_EPS)
        xr = xr * inv * gqr_ref[...]
        xr = xr * cos[:, 0:128] + _swap_quarters(xr, MLA_ROPE // 4) * sin[:, 0:128]
        xr = xr * Q_SCALE
        halves = (jnp.where(low, xr, 0.0), jnp.where(low, pltpu.roll(xr, MLA_ROPE, 1), 0.0))
        for e in range(2):
            h = 2 * j + e
            qn = _rms_rows(q[:, h * MLA_NOPE:(h + 1) * MLA_NOPE], gqn_ref[...], MLA_NOPE) * Q_SCALE
            q_ref[0, h, 0:MLA_NOPE, :] = jnp.transpose(qn).astype(q_ref.dtype)
            q_ref[0, h, MLA_NOPE:, :] = jnp.transpose(halves[e]).astype(q_ref.dtype)

    ones_row = jnp.where(lax.broadcasted_iota(jnp.int32, (MLA_VT_PAD - MLA_V, tm), 0) == 0, 1.0, 0.0)
    for h in range(H):
        kn = _rms_rows(kv[:, h * MLA_NOPE:(h + 1) * MLA_NOPE], gkn_ref[...], MLA_NOPE)
        k_ref[0, h, :, 0:MLA_NOPE] = kn.astype(k_ref.dtype)
        k_ref[0, h, :, MLA_NOPE:] = k_rope.astype(k_ref.dtype)
        v = kv[:, H * MLA_NOPE + h * MLA_V:H * MLA_NOPE + (h + 1) * MLA_V]
        v_ref[0, h, 0:MLA_V, :] = jnp.transpose(v).astype(v_ref.dtype)
        v_ref[0, h, MLA_V:, :] = ones_row.astype(v_ref.dtype)


def _mla_prep(p3, wq, wkv, q_norm, kv_norm, g_qn, g_qr2, g_kn, g_kr2, cos_t, sin_t):
    B, S, _ = p3.shape
    H = MLA_HEADS
    tm = 256
    full = lambda shape: pl.BlockSpec(shape, lambda b, i: (0,) * len(shape))
    return pl.pallas_call(
        _mla_prep_kernel,
        grid=(B, S // tm),
        in_specs=[pl.BlockSpec((1, tm, 1024), lambda b, i: (b, i, COL_TAIL // 1024)),
                  full(wq.shape), full(wkv.shape),
                  full((1, Q_LORA)), full((1, KV_LORA)), full((1, MLA_NOPE)), full((1, 128)),
                  full((1, MLA_NOPE)), full((1, 128)),
                  pl.BlockSpec((tm, 128), lambda b, i: (i, 0)),
                  pl.BlockSpec((tm, 128), lambda b, i: (i, 0))],
        out_specs=[pl.BlockSpec((1, H, MLA_QK_PAD, tm), lambda b, i: (b, 0, 0, i)),
                   pl.BlockSpec((1, H, tm, MLA_QK_PAD), lambda b, i: (b, 0, i, 0)),
                   pl.BlockSpec((1, H, MLA_VT_PAD, tm), lambda b, i: (b, 0, 0, i))],
        out_shape=[jax.ShapeDtypeStruct((B, H, MLA_QK_PAD, S), MXU_DT),
                   jax.ShapeDtypeStruct((B, H, S, MLA_QK_PAD), MXU_DT),
                   jax.ShapeDtypeStruct((B, H, MLA_VT_PAD, S), MXU_DT)],
        compiler_params=_params(("arbitrary", "arbitrary")),
        name="mla_prep",
    )(p3, wq, wkv, q_norm.reshape(1, -1), kv_norm.reshape(1, -1), g_qn.reshape(1, -1), g_qr2,
      g_kn.reshape(1, -1), g_kr2, cos_t, sin_t)


ATT_TQ = 256
ATT_TK = 256
ATT_HP = 2


def _attn_kernel(qt_ref, qtn_ref, k_ref, vt_ref, g_ref, o_ref, s_s, m_s, *, n_first, n_all_tiles):
    tq, tk = ATT_TQ, ATT_TK
    i = pl.program_id(2)
    heads = range(ATT_HP)

    def scores(qt, h, j):
        s = jnp.dot(k_ref[0, h, j * tk:(j + 1) * tk, :], qt, preferred_element_type=F32)
        s_s[h, j * tk:(j + 1) * tk, :] = s
        mm = s[0:8, :]
        for r in range(1, tk // 8):
            mm = jnp.maximum(mm, s[r * 8:(r + 1) * 8, :])
        return mm

    def set_query_max(mm, h):
        m_s[h] = jnp.broadcast_to(jnp.max(mm, axis=0, keepdims=True), (8, tq))

    def stage(n_kv):
        for h in heads:
            qt = qt_ref[0, h]
            mm = scores(qt, h, 0)
            for j in range(1, n_kv):
                mm = jnp.maximum(mm, scores(qt, h, j))
            set_query_max(mm, h)

    def attend(n_cur, n_next):
        m = [m_s[h, 0:1, :] for h in heads]
        qtn = [qtn_ref[0, h] for h in heads]
        acc = [jnp.zeros((MLA_VT_PAD, tq), F32) for _ in heads]
        mm = [None for _ in heads]
        for j in range(max(n_cur, n_next)):
            for h in heads:
                if j < n_cur:
                    p = jnp.exp2(s_s[h, j * tk:(j + 1) * tk, :] - m[h]).astype(MXU_DT)
                    acc[h] = acc[h] + jnp.dot(vt_ref[0, h, :, j * tk:(j + 1) * tk], p,
                                              preferred_element_type=F32)
                if j < n_next:
                    c = scores(qtn[h], h, j)
                    mm[h] = c if mm[h] is None else jnp.maximum(mm[h], c)
        for h in heads:
            set_query_max(mm[h], h)
            out_t = acc[h][0:MLA_V, :] / acc[h][MLA_V:MLA_V + 1, :]
            cs = slice(h * MLA_V, (h + 1) * MLA_V)
            o_ref[0, :, cs] = (jnp.transpose(out_t) * _silu(g_ref[0, :, cs].astype(F32))).astype(o_ref.dtype)

    pl.when(i == 0)(lambda: stage(n_first))
    if n_first == n_all_tiles:
        attend(n_all_tiles, n_all_tiles)
    else:
        pl.when(i == 0)(lambda: attend(n_first, n_all_tiles))
        pl.when(i > 0)(lambda: attend(n_all_tiles, n_all_tiles))


def _mla_attention(qt, kc, vt, p3, n_ctx_rows, with_ctx):
    B, H, S, _ = kc.shape
    tq, HP = ATT_TQ, ATT_HP
    assert n_ctx_rows == tq and ATT_TK == tq and H % HP == 0
    n_all_tiles = S // ATT_TK
    q_off = 0 if with_ctx else 1
    n_q = S // tq - q_off
    n_first = n_ctx_rows // ATT_TK if with_ctx else n_all_tiles
    gc0 = COL_GC // (HP * MLA_V)
    return pl.pallas_call(
        functools.partial(_attn_kernel, n_first=n_first, n_all_tiles=n_all_tiles),
        grid=(B, H // HP, n_q),
        in_specs=[pl.BlockSpec((1, HP, MLA_QK_PAD, tq), lambda b, h, i: (b, h, 0, i + q_off)),
                  pl.BlockSpec((1, HP, MLA_QK_PAD, tq),
                               lambda b, h, i: (b, h, 0, jnp.minimum(i + 1, n_q - 1) + q_off)),
                  pl.BlockSpec((1, HP, S, MLA_QK_PAD), lambda b, h, i: (b, h, 0, 0)),
                  pl.BlockSpec((1, HP, MLA_VT_PAD, S), lambda b, h, i: (b, h, 0, 0)),
                  pl.BlockSpec((1, tq, HP * MLA_V), lambda b, h, i: (b, i + q_off, gc0 + h))],
        out_specs=pl.BlockSpec((1, tq, HP * MLA_V), lambda b, h, i: (b, i, h)),
        out_shape=jax.ShapeDtypeStruct((B, n_q * tq, H * MLA_V), MXU_DT),
        scratch_shapes=[pltpu.VMEM((HP, S, tq), F32), pltpu.VMEM((HP, 8, tq), F32)],
        compiler_params=_params(("arbitrary", "arbitrary", "arbitrary")),
        name="mla_attention",
    )(qt, qt, kc, vt, p3)


def _merge_kernel(ya_ref, yb_ref, yc_ref, ma_ref, mb_ref, mc_ref, x_ref, mod_ref,
                  wa_ref, wb_ref, wc_ref, wo_ref, o_ref):
    def down(y_ref, m_ref, w_ref):
        return jax.nn.sigmoid(m_ref[0].astype(F32)) * jnp.dot(y_ref[0], w_ref[...], preferred_element_type=F32)

    z = down(ya_ref, ma_ref, wa_ref) + down(yb_ref, mb_ref, wb_ref) + down(yc_ref, mc_ref, wc_ref)
    out = jnp.dot(z.astype(MXU_DT), wo_ref[...], preferred_element_type=F32)
    o_ref[0] = x_ref[0] + mod_ref[0, 2:3] * out


def _merge(ya, yb, yc, p3, xs, modrows, wa, wb, wc, wo, row_off):
    B, S, _ = xs.shape
    tm = SUBTILE
    o = row_off // tm
    n_sub = S // tm
    act = pl.BlockSpec((1, tm, D_MODEL), lambda b, i: (b, i + o, 0))
    o_c = o - (S - yc.shape[1]) // tm
    act_c = pl.BlockSpec((1, tm, D_MODEL), lambda b, i: (b, i + o_c, 0))
    gm = lambda k: pl.BlockSpec((1, tm, D_MODEL), lambda b, i: (b, i + o, COL_GM // D_MODEL + k))
    wspec = pl.BlockSpec((D_MODEL, D_MODEL), lambda b, i: (0, 0))
    return pl.pallas_call(
        _merge_kernel,
        grid=(B, n_sub - o),
        in_specs=[act, act, act_c, gm(0), gm(1), gm(2), act,
                  pl.BlockSpec((1, 8, D_MODEL), lambda b, i: (b * n_sub + i + o, 0, 0)),
                  wspec, wspec, wspec, wspec],
        out_specs=pl.BlockSpec((1, tm, D_MODEL), lambda b, i: (b, i, 0)),
        out_shape=jax.ShapeDtypeStruct((B, S - row_off, D_MODEL), F32),
        compiler_params=_params(("arbitrary", "arbitrary")),
        name="merge",
    )(ya, yb, yc, p3, p3, p3, xs, modrows, wa, wb, wc, wo)


def _rope_tables(n_ctx_rows, n_lat_rows, dim):
    rows = n_lat_rows // GRID_W
    row = np.repeat(np.arange(rows), GRID_W).astype(np.float64)
    col = np.tile(np.arange(GRID_W), rows).astype(np.float64)
    n_freq = dim // 4
    inv = ROPE_BASE ** (-np.arange(n_freq, dtype=np.float64) / n_freq)
    ang_r = row[:, None] * inv[None, :]
    ang_c = col[:, None] * inv[None, :]
    ang = np.concatenate([ang_r, ang_r, ang_c, ang_c], axis=-1)
    sign = np.tile(np.concatenate([-np.ones(n_freq), np.ones(n_freq)]), 2)
    cos = np.concatenate([np.ones((n_ctx_rows, dim)), np.cos(ang)], axis=0)
    sin = np.concatenate([np.zeros((n_ctx_rows, dim)), np.sin(ang) * sign], axis=0)
    return cos.astype(np.float32), sin.astype(np.float32)


def _permute_w_in(w):
    xa_to_gb = w[..., :6144]
    tail = w[..., 6144:6848]
    gc_gm = w[..., 6848:]
    pad = jnp.zeros(w.shape[:-1] + (P_COLS - w.shape[-1],), w.dtype)
    return jnp.concatenate([xa_to_gb, gc_gm, tail, pad], axis=-1).astype(MXU_DT)


def _split_heads(w, first):
    K = w.shape[0]
    w3 = w.reshape(K, MLA_HEADS, -1)
    return jnp.concatenate([w3[:, :, :first].reshape(K, -1), w3[:, :, first:].reshape(K, -1)],
                           axis=1).astype(MXU_DT)


def _pad_lanes(g):
    return jnp.concatenate([g, jnp.zeros((128 - g.shape[0],), g.dtype)]).reshape(1, 128)


def kernel(x, c, ctx, c_ctx, w_mod, b_mod, norm_gain, w_in, conv_w, conv_b, lru_wa, lru_ba, lru_wx, lru_bx,
           lru_lambda, ret_theta, ret_gain, mla_q_norm, mla_w_q_up, mla_kv_norm, mla_w_kv_up,
           mla_g_qn, mla_g_qr, mla_g_kn, mla_g_kr, w_down_a, w_down_b, w_down_c, w_out):
    B, T, D = x.shape
    L = ctx.shape[1]
    S = L + T
    depth = w_in.shape[0]
    assert D == D_MODEL and L % SUBTILE == 0 and T % SUBTILE == 0 and T % GRID_W == 0

    cos_r, sin_r = _rope_tables(L, T, RET_HEAD_DIM)
    cos_m, sin_m = _rope_tables(L, T, MLA_ROPE)
    pad64 = lambda t: np.concatenate([t, t], axis=1)
    cos_m, sin_m = pad64(cos_m), pad64(sin_m)

    n_rows = ((B + 1 + 7) // 8) * 8
    cc = jnp.zeros((n_rows, D), F32).at[:B].set(c).at[B].set(c_ctx)
    n_sub = S // SUBTILE
    sub = jnp.arange(B * n_sub)
    mod_row = jnp.where(sub % n_sub < L // SUBTILE, B, sub // n_sub)

    xs = jnp.concatenate([ctx, x], axis=1)
    w_in_p = _permute_w_in(w_in)
    for l in range(depth):
        last = l == depth - 1
        mod = _modulation(cc, w_mod, b_mod, l)
        mod3 = mod.reshape(n_rows, 3, D)[mod_row]
        modrows = jnp.concatenate([mod3, jnp.zeros((B * n_sub, 5, D), F32)], axis=1)

        p3 = _inproj(xs.reshape(B * S, D), modrows, norm_gain[l], w_in_p, l).reshape(B, S, P_COLS)

        w_cat = jnp.concatenate([lru_wa[l], lru_wx[l]], axis=-1).astype(MXU_DT)
        ya = _lru_branch(p3, conv_w[l], conv_b[l], w_cat, lru_ba[l], lru_bx[l], lru_lambda[l], L)
        yb = _ret_branch(p3, cos_r, sin_r, ret_theta[l], ret_gain[l], L)
        qc, kc, vc = _mla_prep(p3, _split_heads(mla_w_q_up[l], MLA_NOPE), _split_heads(mla_w_kv_up[l], MLA_NOPE),
                               mla_q_norm[l], mla_kv_norm[l], mla_g_qn[l],
                               jnp.concatenate([mla_g_qr[l], mla_g_qr[l]]).reshape(1, 128),
                               mla_g_kn[l], _pad_lanes(mla_g_kr[l]), cos_m, sin_m)
        yc = _mla_attention(qc, kc, vc, p3, L, not last)
        xs = _merge(ya, yb, yc, p3, xs, modrows,
                    w_down_a[l].astype(MXU_DT), w_down_b[l].astype(MXU_DT), w_down_c[l].astype(MXU_DT),
                    w_out[l].astype(MXU_DT), L if last else 0)
    return xs
```

```python
import functools
import math

import jax
import jax.numpy as jnp
import numpy as np
from jax import lax
from jax.experimental import pallas as pl
from jax.experimental.pallas import tpu as pltpu

D_MODEL = 1024
GRID_W = 64
ROPE_BASE = 10000.0
NORM_EPS = 1e-6

LRU_BLOCKS = 8
LRU_BLOCK_W = D_MODEL // LRU_BLOCKS
CONV_W = 4
LRU_C = 8.0

RET_HEADS = 4
RET_HEAD_DIM = D_MODEL // RET_HEADS
RET_CHUNK = 128

MLA_HEADS = 8
MLA_NOPE = 128
MLA_ROPE = 64
MLA_V = 128
Q_LORA = 384
KV_LORA = 256
MLA_SCALE = (MLA_NOPE + MLA_ROPE) ** -0.5
Q_SCALE = MLA_SCALE * math.log2(math.e)
MLA_VT_PAD = 144
MLA_QK_PAD = 256

P_COLS = 11 * 1024
COL_XA, COL_GA, COL_QB, COL_KB, COL_VB, COL_GB, COL_GC, COL_GM, COL_TAIL = (
    0, 1024, 2048, 3072, 4096, 5120, 6144, 7168, 10240)

MXU_DT = jnp.bfloat16
P_DT = jnp.bfloat16
SUBTILE = 256
VMEM_LIMIT = 56 * 1024 * 1024
F32 = jnp.float32


def _params(sem):
    return pltpu.CompilerParams(dimension_semantics=sem, vmem_limit_bytes=VMEM_LIMIT)


def _silu(x):
    return x * jax.nn.sigmoid(x)


def _softplus(x):
    return jnp.maximum(x, 0.0) + jnp.log1p(jnp.exp(-jnp.abs(x)))


def _pick(n, cands):
    for c in cands:
        if n % c == 0:
            return c
    raise ValueError(f"no tile in {cands} divides {n}")


def _mod_kernel(c_ref, w_ref, b_ref, o_ref):
    a = _silu(c_ref[...])
    o_ref[...] = jnp.dot(a, w_ref[...], preferred_element_type=F32) + b_ref[...]


def _modulation(cc, w_mod, b_mod, layer):
    rows = cc.shape[0]
    depth, _, n3 = w_mod.shape
    tn = 1024
    return pl.pallas_call(
        _mod_kernel,
        grid=(n3 // tn,),
        in_specs=[pl.BlockSpec((rows, D_MODEL), lambda j: (0, 0)),
                  pl.BlockSpec((None, D_MODEL, tn), lambda j: (layer, 0, j)),
                  pl.BlockSpec((None, 1, tn), lambda j: (layer, 0, j))],
        out_specs=pl.BlockSpec((rows, tn), lambda j: (0, j)),
        out_shape=jax.ShapeDtypeStruct((rows, n3), F32),
        compiler_params=_params(("arbitrary",)),
        name="modulation",
    )(cc, w_mod, b_mod.reshape(depth, 1, n3))


def _inproj_kernel(x_ref, mod_ref, gain_ref, w_ref, o_ref, h_ref, *, tm):
    @pl.when(pl.program_id(1) == 0)
    def _():
        gain = gain_ref[...]
        for s in range(tm // SUBTILE):
            rows = slice(s * SUBTILE, (s + 1) * SUBTILE)
            x = x_ref[rows, :]
            y = x * lax.rsqrt(jnp.mean(x * x, axis=-1, keepdims=True) + NORM_EPS) * gain
            h_ref[rows, :] = (y * (1.0 + mod_ref[s, 1:2, :]) + mod_ref[s, 0:1, :]).astype(h_ref.dtype)

    o_ref[...] = jnp.dot(h_ref[...], w_ref[...], preferred_element_type=F32).astype(o_ref.dtype)


def _inproj(xs2, modrows, gain, w_in_p):
    n = xs2.shape[0]
    tm = _pick(n, (1024, 512, 256))
    tn = P_COLS // 4
    return pl.pallas_call(
        functools.partial(_inproj_kernel, tm=tm),
        grid=(n // tm, P_COLS // tn),
        in_specs=[pl.BlockSpec((tm, D_MODEL), lambda i, j: (i, 0)),
                  pl.BlockSpec((tm // SUBTILE, 8, D_MODEL), lambda i, j: (i, 0, 0)),
                  pl.BlockSpec((1, D_MODEL), lambda i, j: (0, 0)),
                  pl.BlockSpec((D_MODEL, tn), lambda i, j: (0, j))],
        out_specs=pl.BlockSpec((tm, tn), lambda i, j: (i, j)),
        out_shape=jax.ShapeDtypeStruct((n, P_COLS), P_DT),
        scratch_shapes=[pltpu.VMEM((tm, D_MODEL), MXU_DT)],
        compiler_params=_params(("arbitrary", "arbitrary")),
        name="inproj",
    )(xs2, modrows, gain.reshape(1, D_MODEL), w_in_p)


LRU_CHUNK = 64
LRU_NB = 4
SCAN_SEGS = 8
LRU_HALO = 16
LRU_UNROLL = 4


def _lru_kernel(xa_ref, ga_ref, cw_ref, cb_ref, w_ref, ba_ref, bx_ref, lam_ref,
                o_ref, u_s, hf_s, xe_s, as_s, bs_s, hs_s, ac_s, ho_s, e_s, *, n_ctx, n_lat):
    C = LRU_CHUNK
    W = LRU_BLOCK_W
    WB = LRU_NB * W
    SEG = C // SCAN_SEGS
    S = (n_ctx + n_lat) * C
    cw = cw_ref[...]
    cb = cb_ref[...]
    sub = lax.broadcasted_iota(jnp.int32, (SCAN_SEGS, W), 0)

    def conv_chunk(start, first, last):
        E = LRU_HALO
        x = xa_ref[0, pl.ds(start, C), :].astype(F32)
        prev = xa_ref[0, pl.ds(pl.multiple_of(jnp.maximum(start - E, 0), E), E), :].astype(F32)
        nxt = xa_ref[0, pl.ds(pl.multiple_of(jnp.minimum(start + C, S - E), E), E), :].astype(F32)
        xe_s[0:E, :] = jnp.where(first, 0.0, prev)
        xe_s[E:E + C, :] = x
        xe_s[E + C:, :] = jnp.where(last, 0.0, nxt)
        return (xe_s[E - 2:E - 2 + C, :] * cw[0:1] + xe_s[E - 1:E - 1 + C, :] * cw[1:2]
                + x * cw[2:3] + xe_s[E + 1:E + 1 + C, :] * cw[3:4] + cb)

    def coeffs(u, d):
        um = u.astype(MXU_DT)
        z = [jnp.dot(um[:, n * W:(n + 1) * W], w_ref[d, n], preferred_element_type=F32) for n in range(LRU_NB)]
        za = jnp.concatenate([zn[:, :W] for zn in z], axis=1)
        zx = jnp.concatenate([zn[:, W:] for zn in z], axis=1)
        r = jax.nn.sigmoid(za + ba_ref[d:d + 1, :])
        i = jax.nn.sigmoid(zx + bx_ref[d:d + 1, :])
        log_a = (-LRU_C * r) * _softplus(-lam_ref[d:d + 1, :])
        a = jnp.exp(log_a)
        g = -jnp.tanh(log_a) * (1.0 + a * a)
        root = jnp.where(g > 0.0, g * lax.rsqrt(g), 0.0)
        return a, root * (i * u)

    def scan_block(n, h, reverse):
        hk = ak = None
        for k in (range(SEG - 1, -1, -1) if reverse else range(SEG)):
            a_k = as_s[n, pl.ds(k, SCAN_SEGS, stride=SEG), :]
            b_k = bs_s[n, pl.ds(k, SCAN_SEGS, stride=SEG), :]
            if hk is None:
                hk, ak = b_k, a_k
            else:
                hk = a_k * hk + b_k
                ak = a_k * ak
            hs_s[n, k] = hk
            ac_s[n, k] = ak
        s = 1
        while s < SCAN_SEGS:
            shift = SCAN_SEGS - s if reverse else s
            m = sub < SCAN_SEGS - s if reverse else sub >= s
            h_o = pltpu.roll(hk, shift, 0)
            a_o = pltpu.roll(ak, shift, 0)
            hk = jnp.where(m, ak * h_o + hk, hk)
            ak = jnp.where(m, ak * a_o, ak)
            s *= 2
        end = hk + ak * h
        e_s[n] = end
        if reverse:
            cin = jnp.where(sub == SCAN_SEGS - 1, h, pltpu.roll(end, SCAN_SEGS - 1, 0))
            h_new = e_s[n, 0:1, :]
        else:
            cin = jnp.where(sub == 0, h, pltpu.roll(end, 1, 0))
            h_new = e_s[n, SCAN_SEGS - 1:SCAN_SEGS, :]
        for k in range(SEG):
            ho_s[n, pl.ds(k, SCAN_SEGS, stride=SEG), :] = hs_s[n, k] + ac_s[n, k] * cin
        return h_new

    def scan_chunk(a, b, h, reverse):
        for n in range(LRU_NB):
            as_s[n] = a[:, n * W:(n + 1) * W]
            bs_s[n] = b[:, n * W:(n + 1) * W]
        h_new = [scan_block(n, h[:, n * W:(n + 1) * W], reverse) for n in range(LRU_NB)]
        return (jnp.concatenate([ho_s[n] for n in range(LRU_NB)], axis=1),
                jnp.concatenate(h_new, axis=1))

    zero_h = jnp.zeros((1, WB), F32)

    def fwd_body(c, h):
        start = pl.multiple_of(c * C, C)
        first = jnp.logical_or(c == 0, c == n_ctx)
        last = jnp.logical_or(c == n_ctx - 1, c == n_ctx + n_lat - 1)
        u = conv_chunk(start, first, last)
        u_s[pl.ds(start, C), :] = u
        a, b = coeffs(u, 0)
        hfull, h = scan_chunk(a, b, h, False)
        hf_s[pl.ds(start, C), :] = hfull
        return h

    lax.fori_loop(0, n_ctx + n_lat, fwd_body, zero_h, unroll=LRU_UNROLL)

    def bwd_body(c, h):
        start = pl.multiple_of(c * C, C)
        u = u_s[pl.ds(start, C), :]
        a, b = coeffs(u, 1)
        hfull, h = scan_chunk(a, b, h, True)
        y = hf_s[pl.ds(start, C), :] + hfull
        o_ref[0, pl.ds(start, C), :] = (y * _silu(ga_ref[0, pl.ds(start, C), :].astype(F32))).astype(o_ref.dtype)
        return h

    h = lax.fori_loop(0, n_ctx, lambda i, h: bwd_body(n_ctx - 1 - i, h), zero_h, unroll=LRU_UNROLL)
    lax.fori_loop(0, n_lat, lambda i, h: bwd_body(n_ctx + n_lat - 1 - i, h), h, unroll=LRU_UNROLL)


def _lru_branch(p3, conv_w, conv_b, w_cat, ba, bx, lam, n_ctx_rows):
    B, S, _ = p3.shape
    C, W, NB = LRU_CHUNK, LRU_BLOCK_W, LRU_NB
    WB = NB * W
    n_ctx, n_lat = n_ctx_rows // C, (S - n_ctx_rows) // C
    xa0, ga0 = COL_XA // WB, COL_GA // WB
    per_block = lambda rows: pl.BlockSpec((rows, WB), lambda b, n: (0, n))
    return pl.pallas_call(
        functools.partial(_lru_kernel, n_ctx=n_ctx, n_lat=n_lat),
        grid=(B, LRU_BLOCKS // NB),
        in_specs=[pl.BlockSpec((1, S, WB), lambda b, n: (b, 0, xa0 + n)),
                  pl.BlockSpec((1, S, WB), lambda b, n: (b, 0, ga0 + n)),
                  per_block(CONV_W), per_block(1),
                  pl.BlockSpec((2, NB, W, 2 * W), lambda b, n: (0, n, 0, 0)),
                  per_block(2), per_block(2), per_block(2)],
        out_specs=pl.BlockSpec((1, S, WB), lambda b, n: (b, 0, n)),
        out_shape=jax.ShapeDtypeStruct((B, S, D_MODEL), MXU_DT),
        scratch_shapes=[pltpu.VMEM((S, WB), F32), pltpu.VMEM((S, WB), F32),
                        pltpu.VMEM((C + 2 * LRU_HALO, WB), F32), pltpu.VMEM((NB, C, W), F32), pltpu.VMEM((NB, C, W), F32),
                        pltpu.VMEM((NB, C // SCAN_SEGS, SCAN_SEGS, W), F32),
                        pltpu.VMEM((NB, C // SCAN_SEGS, SCAN_SEGS, W), F32),
                        pltpu.VMEM((NB, C, W), F32), pltpu.VMEM((NB, SCAN_SEGS, W), F32)],
        compiler_params=_params(("arbitrary", "arbitrary")),
        name="lru",
    )(p3, p3, conv_w, conv_b.reshape(1, D_MODEL), w_cat, ba, bx, lam)


def _swap_quarters(x, q):
    lane = lax.broadcasted_iota(jnp.int32, (x.shape[0], 128), 1)
    cols = []
    for c in range(x.shape[1] // 128):
        xc = x[:, c * 128:(c + 1) * 128]
        if 2 * q == 128:
            cols.append(pltpu.roll(xc, q, 1))
        else:
            cols.append(jnp.where(lane % (2 * q) < q, pltpu.roll(xc, 128 - q, 1), pltpu.roll(xc, q, 1)))
    return cols[0] if len(cols) == 1 else jnp.concatenate(cols, axis=1)


RET_SUB = 2


def _ret_block_index(d, c, n_ctx, n_all):
    rev = jnp.where(c < n_ctx, n_ctx - 1 - c, n_ctx + n_all - 1 - c)
    return jnp.where(d == 0, c, rev)


def _ret_kernel(q_ref, k_ref, v_ref, g_ref, cos_ref, sin_ref, theta_ref, gain_ref,
                o_ref, s_s, of_s, oc_s, inner_s, dec_s, *, n_ctx, n_all):
    C, dh = RET_CHUNK, RET_HEAD_DIM
    RB = RET_SUB * C
    d = pl.program_id(1)
    c = pl.program_id(2)
    start = pl.multiple_of(_ret_block_index(d, c, n_ctx, n_all) * RB, RB)
    fwd = d == 0
    wide = lambda t: jnp.concatenate([t] * (dh // C), axis=1)

    @pl.when(c == 0)
    def _():
        s_s[...] = jnp.zeros_like(s_s)
        row = lax.broadcasted_iota(jnp.int32, (C, C), 0)
        col = lax.broadcasted_iota(jnp.int32, (C, C), 1)
        diff = jnp.where(fwd, row - col, col - row).astype(F32)
        q_pow = jnp.where(fwd, row + 1, C - row).astype(F32)
        k_pow = jnp.where(fwd, C - 1 - row, row).astype(F32)
        for h in range(RET_HEADS):
            lg = -_softplus(-theta_ref[pl.ds(d * RET_HEADS + h, 1), :])
            inner_s[h] = jnp.where(diff >= 0, jnp.exp(lg * jnp.maximum(diff, 0.0)), 0.0)
            dec_s[h, 0] = jnp.exp(lg * q_pow)
            dec_s[h, 1] = jnp.exp(lg * k_pow)
            dec_s[h, 2] = jnp.broadcast_to(jnp.exp(lg * float(C)), (C, C))

    k_scale = dh ** -0.5
    for t in range(RET_SUB):
        r0 = pl.multiple_of(jnp.where(fwd, t, RET_SUB - 1 - t) * C, C)
        rows = pl.ds(r0, C)
        cos = cos_ref[rows, :]
        sin = sin_ref[rows, :]
        for h in range(RET_HEADS):
            cs = slice(h * dh, (h + 1) * dh)
            q = q_ref[0, rows, cs].astype(F32)
            k = k_ref[0, rows, cs].astype(F32)
            q = q * cos + _swap_quarters(q, dh // 4) * sin
            k = (k * cos + _swap_quarters(k, dh // 4) * sin) * k_scale
            qm = q.astype(MXU_DT)
            km = k.astype(MXU_DT)
            vm = v_ref[0, rows, cs].astype(MXU_DT)
            s_old = s_s[h]
            att = lax.dot_general(qm, km, (((1,), (1,)), ((), ())), preferred_element_type=F32) * inner_s[h]
            oc_s[rows, cs] = (jnp.dot(att.astype(MXU_DT), vm, preferred_element_type=F32)
                              + jnp.dot(qm, s_old.astype(MXU_DT), preferred_element_type=F32) * wide(dec_s[h, 0]))
            kt = jnp.transpose(k * wide(dec_s[h, 1])).astype(MXU_DT)
            s_s[h] = s_old * wide(dec_s[h, 2, 0:1, :]) + jnp.dot(kt, vm, preferred_element_type=F32)

    @pl.when(fwd)
    def _():
        of_s[pl.ds(start, RB), :] = oc_s[...]

    @pl.when(jnp.logical_not(fwd))
    def _():
        for h in range(RET_HEADS):
            cs = slice(h * dh, (h + 1) * dh)
            t = oc_s[:, cs] + of_s[pl.ds(start, RB), cs]
            tc = t - jnp.mean(t, axis=-1, keepdims=True)
            y = tc * lax.rsqrt(jnp.mean(tc * tc, axis=-1, keepdims=True) + NORM_EPS) * gain_ref[:, cs]
            o_ref[0, :, cs] = (y * _silu(g_ref[0, :, cs].astype(F32))).astype(o_ref.dtype)


def _ret_branch(p3, cos_t, sin_t, theta, gain, n_ctx_rows):
    B, S, _ = p3.shape
    C = RET_CHUNK
    RB = RET_SUB * C
    assert n_ctx_rows % RB == 0 and S % RB == 0
    n_ctx, n_all = n_ctx_rows // RB, S // RB
    bidx = functools.partial(_ret_block_index, n_ctx=n_ctx, n_all=n_all)

    def pcol(col):
        return pl.BlockSpec((1, RB, D_MODEL), lambda b, d, c: (b, bidx(d, c), col // D_MODEL))

    def out_map(b, d, c):
        return (b, jnp.where(d == 0, bidx(1, 0), bidx(d, c)), 0)

    return pl.pallas_call(
        functools.partial(_ret_kernel, n_ctx=n_ctx, n_all=n_all),
        grid=(B, 2, n_all),
        in_specs=[pcol(COL_QB), pcol(COL_KB), pcol(COL_VB), pcol(COL_GB),
                  pl.BlockSpec((RB, RET_HEAD_DIM), lambda b, d, c: (bidx(d, c), 0)),
                  pl.BlockSpec((RB, RET_HEAD_DIM), lambda b, d, c: (bidx(d, c), 0)),
                  pl.BlockSpec((2 * RET_HEADS, C), lambda b, d, c: (0, 0)),
                  pl.BlockSpec((1, D_MODEL), lambda b, d, c: (0, 0))],
        out_specs=pl.BlockSpec((1, RB, D_MODEL), out_map),
        out_shape=jax.ShapeDtypeStruct((B, S, D_MODEL), MXU_DT),
        scratch_shapes=[pltpu.VMEM((RET_HEADS, RET_HEAD_DIM, RET_HEAD_DIM), F32),
                        pltpu.VMEM((S, D_MODEL), F32), pltpu.VMEM((RB, D_MODEL), F32),
                        pltpu.VMEM((RET_HEADS, C, C), F32), pltpu.VMEM((RET_HEADS, 3, C, C), F32)],
        compiler_params=_params(("arbitrary", "arbitrary", "arbitrary")),
        name="retention",
    )(p3, p3, p3, p3, cos_t, sin_t, jnp.broadcast_to(theta.reshape(2 * RET_HEADS, 1), (2 * RET_HEADS, C)),
      gain.reshape(1, D_MODEL))


def _rms_rows(x, gain, width):
    return x * lax.rsqrt(jnp.sum(x * x, axis=-1, keepdims=True) * (1.0 / width) + NORM_EPS) * gain


def _mla_prep_kernel(p_ref, wq_ref, wkv_ref, qn_ref, kvn_ref, gqn_ref, gqr_ref, gkn_ref, gkr_ref,
                     cos_ref, sin_ref, q_ref, k_ref, v_ref):
    H = MLA_HEADS
    blk = p_ref[0].astype(F32)
    qd = blk[:, 0:Q_LORA]
    kvd = blk[:, Q_LORA:Q_LORA + KV_LORA]
    kr = blk[:, Q_LORA + KV_LORA:Q_LORA + KV_LORA + 128]
    tm = blk.shape[0]
    lane = lax.broadcasted_iota(jnp.int32, (tm, 128), 1)
    low = lane < MLA_ROPE
    cos = cos_ref[...]
    sin = sin_ref[...]

    q = jnp.dot(_rms_rows(qd, qn_ref[...], Q_LORA).astype(MXU_DT), wq_ref[...], preferred_element_type=F32)
    kv = jnp.dot(_rms_rows(kvd, kvn_ref[...], KV_LORA).astype(MXU_DT), wkv_ref[...], preferred_element_type=F32)

    k_rope = _rms_rows(kr, gkr_ref[...], MLA_ROPE)
    k_rope = k_rope * cos[:, 0:128] + _swap_quarters(k_rope, MLA_ROPE // 4) * sin[:, 0:128]
    k_rope = jnp.where(low, k_rope, 0.0)

    for j in range(H // 2):
        xr = q[:, H * MLA_NOPE + j * 128:H * MLA_NOPE + (j + 1) * 128]
        sq = xr * xr
        ms_lo = jnp.sum(jnp.where(low, sq, 0.0), axis=-1, keepdims=True)
        ms_hi = jnp.sum(jnp.where(low, 0.0, sq), axis=-1, keepdims=True)
        inv = lax.rsqrt(jnp.where(low, ms_lo, ms_hi) * (1.0 / MLA_ROPE) + NORM_EPS)
        xr = xr * inv * gqr_ref[...]
        xr = xr * cos[:, 0:128] + _swap_quarters(xr, MLA_ROPE // 4) * sin[:, 0:128]
        xr_t = jnp.transpose(xr * Q_SCALE).astype(q_ref.dtype)
        for e in range(2):
            h = 2 * j + e
            qn = _rms_rows(q[:, h * MLA_NOPE:(h + 1) * MLA_NOPE], gqn_ref[...], MLA_NOPE) * Q_SCALE
            q_ref[0, h, 0:MLA_NOPE, :] = jnp.transpose(qn).astype(q_ref.dtype)
            q_ref[0, h, MLA_NOPE:MLA_NOPE + MLA_ROPE, :] = xr_t[e * MLA_ROPE:(e + 1) * MLA_ROPE, :]
            q_ref[0, h, MLA_NOPE + MLA_ROPE:, :] = jnp.zeros((MLA_QK_PAD - MLA_NOPE - MLA_ROPE, tm), q_ref.dtype)

    ones_row = jnp.where(lax.broadcasted_iota(jnp.int32, (MLA_VT_PAD - MLA_V, tm), 0) == 0, 1.0, 0.0)
    for h in range(H):
        kn = _rms_rows(kv[:, h * MLA_NOPE:(h + 1) * MLA_NOPE], gkn_ref[...], MLA_NOPE)
        k_ref[0, h, :, 0:MLA_NOPE] = kn.astype(k_ref.dtype)
        k_ref[0, h, :, MLA_NOPE:] = k_rope.astype(k_ref.dtype)
        v = kv[:, H * MLA_NOPE + h * MLA_V:H * MLA_NOPE + (h + 1) * MLA_V]
        v_ref[0, h, 0:MLA_V, :] = jnp.transpose(v).astype(v_ref.dtype)
        v_ref[0, h, MLA_V:, :] = ones_row.astype(v_ref.dtype)


def _mla_prep(p3, wq, wkv, q_norm, kv_norm, g_qn, g_qr2, g_kn, g_kr2, cos_t, sin_t):
    B, S, _ = p3.shape
    H = MLA_HEADS
    tm = 256
    full = lambda shape: pl.BlockSpec(shape, lambda b, i: (0,) * len(shape))
    return pl.pallas_call(
        _mla_prep_kernel,
        grid=(B, S // tm),
        in_specs=[pl.BlockSpec((1, tm, 1024), lambda b, i: (b, i, COL_TAIL // 1024)),
                  full(wq.shape), full(wkv.shape),
                  full((1, Q_LORA)), full((1, KV_LORA)), full((1, MLA_NOPE)), full((1, 128)),
                  full((1, MLA_NOPE)), full((1, 128)),
                  pl.BlockSpec((tm, 128), lambda b, i: (i, 0)),
                  pl.BlockSpec((tm, 128), lambda b, i: (i, 0))],
        out_specs=[pl.BlockSpec((1, H, MLA_QK_PAD, tm), lambda b, i: (b, 0, 0, i)),
                   pl.BlockSpec((1, H, tm, MLA_QK_PAD), lambda b, i: (b, 0, i, 0)),
                   pl.BlockSpec((1, H, MLA_VT_PAD, tm), lambda b, i: (b, 0, 0, i))],
        out_shape=[jax.ShapeDtypeStruct((B, H, MLA_QK_PAD, S), MXU_DT),
                   jax.ShapeDtypeStruct((B, H, S, MLA_QK_PAD), MXU_DT),
                   jax.ShapeDtypeStruct((B, H, MLA_VT_PAD, S), MXU_DT)],
        compiler_params=_params(("arbitrary", "arbitrary")),
        name="mla_prep",
    )(p3, wq, wkv, q_norm.reshape(1, -1), kv_norm.reshape(1, -1), g_qn.reshape(1, -1), g_qr2,
      g_kn.reshape(1, -1), g_kr2, cos_t, sin_t)


ATT_TQ = 256
ATT_TK = 256
ATT_HP = 2


def _attn_kernel(qt_ref, qtn_ref, k_ref, vt_ref, g_ref, o_ref, s_s, m_s, *, n_first, n_all_tiles):
    tq, tk = ATT_TQ, ATT_TK
    i = pl.program_id(2)
    heads = range(ATT_HP)

    def scores(qt, h, j):
        s = jnp.dot(k_ref[0, h, j * tk:(j + 1) * tk, :], qt, preferred_element_type=F32)
        s_s[h, j * tk:(j + 1) * tk, :] = s
        mm = s[0:8, :]
        for r in range(1, tk // 8):
            mm = jnp.maximum(mm, s[r * 8:(r + 1) * 8, :])
        return mm

    def set_query_max(mm, h):
        m_s[h] = jnp.broadcast_to(jnp.max(mm, axis=0, keepdims=True), (8, tq))

    def stage(n_kv):
        for h in heads:
            qt = qt_ref[0, h]
            mm = scores(qt, h, 0)
            for j in range(1, n_kv):
                mm = jnp.maximum(mm, scores(qt, h, j))
            set_query_max(mm, h)

    def attend(n_cur, n_next):
        m = [m_s[h, 0:1, :] for h in heads]
        qtn = [qtn_ref[0, h] for h in heads]
        acc = [jnp.zeros((MLA_VT_PAD, tq), F32) for _ in heads]
        mm = [None for _ in heads]
        for j in range(max(n_cur, n_next)):
            for h in heads:
                if j < n_cur:
                    p = jnp.exp2(s_s[h, j * tk:(j + 1) * tk, :] - m[h]).astype(MXU_DT)
                    acc[h] = acc[h] + jnp.dot(vt_ref[0, h, :, j * tk:(j + 1) * tk], p,
                                              preferred_element_type=F32)
                if j < n_next:
                    c = scores(qtn[h], h, j)
                    mm[h] = c if mm[h] is None else jnp.maximum(mm[h], c)
        for h in heads:
            set_query_max(mm[h], h)
            out_t = acc[h][0:MLA_V, :] / acc[h][MLA_V:MLA_V + 1, :]
            cs = slice(h * MLA_V, (h + 1) * MLA_V)
            o_ref[0, :, cs] = (jnp.transpose(out_t) * _silu(g_ref[0, :, cs].astype(F32))).astype(o_ref.dtype)

    pl.when(i == 0)(lambda: stage(n_first))
    if n_first == n_all_tiles:
        attend(n_all_tiles, n_all_tiles)
    else:
        pl.when(i == 0)(lambda: attend(n_first, n_all_tiles))
        pl.when(i > 0)(lambda: attend(n_all_tiles, n_all_tiles))


def _mla_attention(qt, kc, vt, p3, n_ctx_rows, with_ctx):
    B, H, S, _ = kc.shape
    tq, HP = ATT_TQ, ATT_HP
    assert n_ctx_rows == tq and ATT_TK == tq and H % HP == 0
    n_all_tiles = S // ATT_TK
    q_off = 0 if with_ctx else 1
    n_q = S // tq - q_off
    n_first = n_ctx_rows // ATT_TK if with_ctx else n_all_tiles
    gc0 = COL_GC // (HP * MLA_V)
    return pl.pallas_call(
        functools.partial(_attn_kernel, n_first=n_first, n_all_tiles=n_all_tiles),
        grid=(B, H // HP, n_q),
        in_specs=[pl.BlockSpec((1, HP, MLA_QK_PAD, tq), lambda b, h, i: (b, h, 0, i + q_off)),
                  pl.BlockSpec((1, HP, MLA_QK_PAD, tq),
                               lambda b, h, i: (b, h, 0, jnp.minimum(i + 1, n_q - 1) + q_off)),
                  pl.BlockSpec((1, HP, S, MLA_QK_PAD), lambda b, h, i: (b, h, 0, 0)),
                  pl.BlockSpec((1, HP, MLA_VT_PAD, S), lambda b, h, i: (b, h, 0, 0)),
                  pl.BlockSpec((1, tq, HP * MLA_V), lambda b, h, i: (b, i + q_off, gc0 + h))],
        out_specs=pl.BlockSpec((1, tq, HP * MLA_V), lambda b, h, i: (b, i, h)),
        out_shape=jax.ShapeDtypeStruct((B, n_q * tq, H * MLA_V), MXU_DT),
        scratch_shapes=[pltpu.VMEM((HP, S, tq), F32), pltpu.VMEM((HP, 8, tq), F32)],
        compiler_params=_params(("arbitrary", "arbitrary", "arbitrary")),
        name="mla_attention",
    )(qt, qt, kc, vt, p3)


def _merge_kernel(ya_ref, yb_ref, yc_ref, ma_ref, mb_ref, mc_ref, x_ref, mod_ref,
                  wa_ref, wb_ref, wc_ref, wo_ref, o_ref):
    def down(y_ref, m_ref, w_ref):
        return jax.nn.sigmoid(m_ref[0].astype(F32)) * jnp.dot(y_ref[0], w_ref[...], preferred_element_type=F32)

    z = down(ya_ref, ma_ref, wa_ref) + down(yb_ref, mb_ref, wb_ref) + down(yc_ref, mc_ref, wc_ref)
    out = jnp.dot(z.astype(MXU_DT), wo_ref[...], preferred_element_type=F32)
    o_ref[0] = x_ref[0] + mod_ref[0, 2:3] * out


def _merge(ya, yb, yc, p3, xs, modrows, wa, wb, wc, wo, row_off):
    B, S, _ = xs.shape
    tm = SUBTILE
    o = row_off // tm
    n_sub = S // tm
    act = pl.BlockSpec((1, tm, D_MODEL), lambda b, i: (b, i + o, 0))
    o_c = o - (S - yc.shape[1]) // tm
    act_c = pl.BlockSpec((1, tm, D_MODEL), lambda b, i: (b, i + o_c, 0))
    gm = lambda k: pl.BlockSpec((1, tm, D_MODEL), lambda b, i: (b, i + o, COL_GM // D_MODEL + k))
    wspec = pl.BlockSpec((D_MODEL, D_MODEL), lambda b, i: (0, 0))
    return pl.pallas_call(
        _merge_kernel,
        grid=(B, n_sub - o),
        in_specs=[act, act, act_c, gm(0), gm(1), gm(2), act,
                  pl.BlockSpec((1, 8, D_MODEL), lambda b, i: (b * n_sub + i + o, 0, 0)),
                  wspec, wspec, wspec, wspec],
        out_specs=pl.BlockSpec((1, tm, D_MODEL), lambda b, i: (b, i, 0)),
        out_shape=jax.ShapeDtypeStruct((B, S - row_off, D_MODEL), F32),
        compiler_params=_params(("arbitrary", "arbitrary")),
        name="merge",
    )(ya, yb, yc, p3, p3, p3, xs, modrows, wa, wb, wc, wo)


def _rope_tables(n_ctx_rows, n_lat_rows, dim):
    rows = n_lat_rows // GRID_W
    row = np.repeat(np.arange(rows), GRID_W).astype(np.float64)
    col = np.tile(np.arange(GRID_W), rows).astype(np.float64)
    n_freq = dim // 4
    inv = ROPE_BASE ** (-np.arange(n_freq, dtype=np.float64) / n_freq)
    ang_r = row[:, None] * inv[None, :]
    ang_c = col[:, None] * inv[None, :]
    ang = np.concatenate([ang_r, ang_r, ang_c, ang_c], axis=-1)
    sign = np.tile(np.concatenate([-np.ones(n_freq), np.ones(n_freq)]), 2)
    cos = np.concatenate([np.ones((n_ctx_rows, dim)), np.cos(ang)], axis=0)
    sin = np.concatenate([np.zeros((n_ctx_rows, dim)), np.sin(ang) * sign], axis=0)
    return cos.astype(np.float32), sin.astype(np.float32)


def _permute_w_in(w):
    xa_to_gb = w[..., :6144]
    tail = w[..., 6144:6848]
    gc_gm = w[..., 6848:]
    pad = jnp.zeros(w.shape[:-1] + (P_COLS - w.shape[-1],), w.dtype)
    return jnp.concatenate([xa_to_gb, gc_gm, tail, pad], axis=-1).astype(MXU_DT)


def _split_heads(w, first):
    K = w.shape[0]
    w3 = w.reshape(K, MLA_HEADS, -1)
    return jnp.concatenate([w3[:, :, :first].reshape(K, -1), w3[:, :, first:].reshape(K, -1)],
                           axis=1).astype(MXU_DT)


def _pad_lanes(g):
    return jnp.concatenate([g, jnp.zeros((128 - g.shape[0],), g.dtype)]).reshape(1, 128)


def kernel(x, c, ctx, c_ctx, w_mod, b_mod, norm_gain, w_in, conv_w, conv_b, lru_wa, lru_ba, lru_wx, lru_bx,
           lru_lambda, ret_theta, ret_gain, mla_q_norm, mla_w_q_up, mla_kv_norm, mla_w_kv_up,
           mla_g_qn, mla_g_qr, mla_g_kn, mla_g_kr, w_down_a, w_down_b, w_down_c, w_out):
    B, T, D = x.shape
    L = ctx.shape[1]
    S = L + T
    depth = w_in.shape[0]
    assert D == D_MODEL and L % SUBTILE == 0 and T % SUBTILE == 0 and T % GRID_W == 0

    cos_r, sin_r = _rope_tables(L, T, RET_HEAD_DIM)
    cos_m, sin_m = _rope_tables(L, T, MLA_ROPE)
    pad64 = lambda t: np.concatenate([t, t], axis=1)
    cos_m, sin_m = pad64(cos_m), pad64(sin_m)

    n_rows = ((B + 1 + 7) // 8) * 8
    cc = jnp.zeros((n_rows, D), F32).at[:B].set(c).at[B].set(c_ctx)
    n_sub = S // SUBTILE
    sub = jnp.arange(B * n_sub)
    mod_row = jnp.where(sub % n_sub < L // SUBTILE, B, sub // n_sub)

    xs = jnp.concatenate([ctx, x], axis=1)
    for l in range(depth):
        last = l == depth - 1
        mod = _modulation(cc, w_mod, b_mod, l)
        mod3 = mod.reshape(n_rows, 3, D)[mod_row]
        modrows = jnp.concatenate([mod3, jnp.zeros((B * n_sub, 5, D), F32)], axis=1)

        p3 = _inproj(xs.reshape(B * S, D), modrows, norm_gain[l], _permute_w_in(w_in[l])).reshape(B, S, P_COLS)

        w_cat = jnp.concatenate([lru_wa[l], lru_wx[l]], axis=-1).astype(MXU_DT)
        ya = _lru_branch(p3, conv_w[l], conv_b[l], w_cat, lru_ba[l], lru_bx[l], lru_lambda[l], L)
        yb = _ret_branch(p3, cos_r, sin_r, ret_theta[l], ret_gain[l], L)
        qc, kc, vc = _mla_prep(p3, _split_heads(mla_w_q_up[l], MLA_NOPE), _split_heads(mla_w_kv_up[l], MLA_NOPE),
                               mla_q_norm[l], mla_kv_norm[l], mla_g_qn[l],
                               jnp.concatenate([mla_g_qr[l], mla_g_qr[l]]).reshape(1, 128),
                               mla_g_kn[l], _pad_lanes(mla_g_kr[l]), cos_m, sin_m)
        yc = _mla_attention(qc, kc, vc, p3, L, not last)
        xs = _merge(ya, yb, yc, p3, xs, modrows,
                    w_down_a[l].astype(MXU_DT), w_down_b[l].astype(MXU_DT), w_down_c[l].astype(MXU_DT),
                    w_out[l].astype(MXU_DT), L if last else 0)
    return xs
```

```python
import functools
import math

import jax
import jax.numpy as jnp
import numpy as np
from jax import lax
from jax.experimental import pallas as pl
from jax.experimental.pallas import tpu as pltpu

D_MODEL = 1024
GRID_W = 64
ROPE_BASE = 10000.0
NORM_EPS = 1e-6

LRU_BLOCKS = 8
LRU_BLOCK_W = D_MODEL // LRU_BLOCKS
CONV_W = 4
LRU_C = 8.0

RET_HEADS = 4
RET_HEAD_DIM = D_MODEL // RET_HEADS
RET_CHUNK = 128

MLA_HEADS = 8
MLA_NOPE = 128
MLA_ROPE = 64
MLA_V = 128
Q_LORA = 384
KV_LORA = 256
MLA_SCALE = (MLA_NOPE + MLA_ROPE) ** -0.5
Q_SCALE = MLA_SCALE * math.log2(math.e)
MLA_VT_PAD = 144
MLA_QK_PAD = 256

P_COLS = 11 * 1024
COL_XA, COL_GA, COL_QB, COL_KB, COL_VB, COL_GB, COL_GC, COL_GM, COL_TAIL = (
    0, 1024, 2048, 3072, 4096, 5120, 6144, 7168, 10240)

MXU_DT = jnp.bfloat16
P_DT = jnp.bfloat16
SUBTILE = 256
VMEM_LIMIT = 56 * 1024 * 1024
F32 = jnp.float32


def _params(sem):
    return pltpu.CompilerParams(dimension_semantics=sem, vmem_limit_bytes=VMEM_LIMIT)


def _silu(x):
    return x * jax.nn.sigmoid(x)


def _softplus(x):
    return jnp.maximum(x, 0.0) + jnp.log1p(jnp.exp(-jnp.abs(x)))


def _pick(n, cands):
    for c in cands:
        if n % c == 0:
            return c
    raise ValueError(f"no tile in {cands} divides {n}")


def _mod_kernel(c_ref, w_ref, b_ref, o_ref):
    a = _silu(c_ref[...])
    o_ref[...] = jnp.dot(a, w_ref[...], preferred_element_type=F32) + b_ref[...]


def _modulation(cc, w_mod, b_mod, layer):
    rows = cc.shape[0]
    depth, _, n3 = w_mod.shape
    tn = 1024
    return pl.pallas_call(
        _mod_kernel,
        grid=(n3 // tn,),
        in_specs=[pl.BlockSpec((rows, D_MODEL), lambda j: (0, 0)),
                  pl.BlockSpec((None, D_MODEL, tn), lambda j: (layer, 0, j)),
                  pl.BlockSpec((None, 1, tn), lambda j: (layer, 0, j))],
        out_specs=pl.BlockSpec((rows, tn), lambda j: (0, j)),
        out_shape=jax.ShapeDtypeStruct((rows, n3), F32),
        compiler_params=_params(("arbitrary",)),
        name="modulation",
    )(cc, w_mod, b_mod.reshape(depth, 1, n3))


def _inproj_kernel(x_ref, mod_ref, gain_ref, w_ref, o_ref, h_ref, *, tm):
    @pl.when(pl.program_id(1) == 0)
    def _():
        gain = gain_ref[...]
        for s in range(tm // SUBTILE):
            rows = slice(s * SUBTILE, (s + 1) * SUBTILE)
            x = x_ref[rows, :]
            y = x * lax.rsqrt(jnp.mean(x * x, axis=-1, keepdims=True) + NORM_EPS) * gain
            h_ref[rows, :] = (y * (1.0 + mod_ref[s, 1:2, :]) + mod_ref[s, 0:1, :]).astype(h_ref.dtype)

    o_ref[...] = jnp.dot(h_ref[...], w_ref[...], preferred_element_type=F32).astype(o_ref.dtype)


def _inproj(xs2, modrows, gain, w_in_p):
    n = xs2.shape[0]
    tm = _pick(n, (1024, 512, 256))
    tn = P_COLS // 4
    return pl.pallas_call(
        functools.partial(_inproj_kernel, tm=tm),
        grid=(n // tm, P_COLS // tn),
        in_specs=[pl.BlockSpec((tm, D_MODEL), lambda i, j: (i, 0)),
                  pl.BlockSpec((tm // SUBTILE, 8, D_MODEL), lambda i, j: (i, 0, 0)),
                  pl.BlockSpec((1, D_MODEL), lambda i, j: (0, 0)),
                  pl.BlockSpec((D_MODEL, tn), lambda i, j: (0, j))],
        out_specs=pl.BlockSpec((tm, tn), lambda i, j: (i, j)),
        out_shape=jax.ShapeDtypeStruct((n, P_COLS), P_DT),
        scratch_shapes=[pltpu.VMEM((tm, D_MODEL), MXU_DT)],
        compiler_params=_params(("arbitrary", "arbitrary")),
        name="inproj",
    )(xs2, modrows, gain.reshape(1, D_MODEL), w_in_p)


LRU_CHUNK = 64
LRU_NB = 4
SCAN_SEGS = 8
LRU_HALO = 16
LRU_UNROLL = 4


def _lru_kernel(xa_ref, ga_ref, cw_ref, cb_ref, w_ref, ba_ref, bx_ref, lam_ref,
                o_ref, u_s, hf_s, xe_s, as_s, bs_s, hs_s, ac_s, ho_s, e_s, *, n_ctx, n_lat):
    C = LRU_CHUNK
    W = LRU_BLOCK_W
    WB = LRU_NB * W
    SEG = C // SCAN_SEGS
    S = (n_ctx + n_lat) * C
    cw = cw_ref[...]
    cb = cb_ref[...]
    sub = lax.broadcasted_iota(jnp.int32, (SCAN_SEGS, W), 0)

    def conv_chunk(start, first, last):
        E = LRU_HALO
        x = xa_ref[0, pl.ds(start, C), :].astype(F32)
        prev = xa_ref[0, pl.ds(pl.multiple_of(jnp.maximum(start - E, 0), E), E), :].astype(F32)
        nxt = xa_ref[0, pl.ds(pl.multiple_of(jnp.minimum(start + C, S - E), E), E), :].astype(F32)
        xe_s[0:E, :] = jnp.where(first, 0.0, prev)
        xe_s[E:E + C, :] = x
        xe_s[E + C:, :] = jnp.where(last, 0.0, nxt)
        return (xe_s[E - 2:E - 2 + C, :] * cw[0:1] + xe_s[E - 1:E - 1 + C, :] * cw[1:2]
                + x * cw[2:3] + xe_s[E + 1:E + 1 + C, :] * cw[3:4] + cb)

    def coeffs(u, d):
        um = u.astype(MXU_DT)
        z = [jnp.dot(um[:, n * W:(n + 1) * W], w_ref[d, n], preferred_element_type=F32) for n in range(LRU_NB)]
        za = jnp.concatenate([zn[:, :W] for zn in z], axis=1)
        zx = jnp.concatenate([zn[:, W:] for zn in z], axis=1)
        r = jax.nn.sigmoid(za + ba_ref[d:d + 1, :])
        i = jax.nn.sigmoid(zx + bx_ref[d:d + 1, :])
        log_a = (-LRU_C * r) * _softplus(-lam_ref[d:d + 1, :])
        a = jnp.exp(log_a)
        g = -jnp.tanh(log_a) * (1.0 + a * a)
        root = jnp.where(g > 0.0, g * lax.rsqrt(g), 0.0)
        return a, root * (i * u)

    def scan_block(n, h, reverse):
        hk = ak = None
        for k in (range(SEG - 1, -1, -1) if reverse else range(SEG)):
            a_k = as_s[n, pl.ds(k, SCAN_SEGS, stride=SEG), :]
            b_k = bs_s[n, pl.ds(k, SCAN_SEGS, stride=SEG), :]
            if hk is None:
                hk, ak = b_k, a_k
            else:
                hk = a_k * hk + b_k
                ak = a_k * ak
            hs_s[n, k] = hk
            ac_s[n, k] = ak
        s = 1
        while s < SCAN_SEGS:
            shift = SCAN_SEGS - s if reverse else s
            m = sub < SCAN_SEGS - s if reverse else sub >= s
            h_o = pltpu.roll(hk, shift, 0)
            a_o = pltpu.roll(ak, shift, 0)
            hk = jnp.where(m, ak * h_o + hk, hk)
            ak = jnp.where(m, ak * a_o, ak)
            s *= 2
        end = hk + ak * h
        e_s[n] = end
        if reverse:
            cin = jnp.where(sub == SCAN_SEGS - 1, h, pltpu.roll(end, SCAN_SEGS - 1, 0))
            h_new = e_s[n, 0:1, :]
        else:
            cin = jnp.where(sub == 0, h, pltpu.roll(end, 1, 0))
            h_new = e_s[n, SCAN_SEGS - 1:SCAN_SEGS, :]
        for k in range(SEG):
            ho_s[n, pl.ds(k, SCAN_SEGS, stride=SEG), :] = hs_s[n, k] + ac_s[n, k] * cin
        return h_new

    def scan_chunk(a, b, h, reverse):
        for n in range(LRU_NB):
            as_s[n] = a[:, n * W:(n + 1) * W]
            bs_s[n] = b[:, n * W:(n + 1) * W]
        h_new = [scan_block(n, h[:, n * W:(n + 1) * W], reverse) for n in range(LRU_NB)]
        return (jnp.concatenate([ho_s[n] for n in range(LRU_NB)], axis=1),
                jnp.concatenate(h_new, axis=1))

    zero_h = jnp.zeros((1, WB), F32)

    def fwd_body(c, h):
        start = pl.multiple_of(c * C, C)
        first = jnp.logical_or(c == 0, c == n_ctx)
        last = jnp.logical_or(c == n_ctx - 1, c == n_ctx + n_lat - 1)
        u = conv_chunk(start, first, last)
        u_s[pl.ds(start, C), :] = u
        a, b = coeffs(u, 0)
        hfull, h = scan_chunk(a, b, h, False)
        hf_s[pl.ds(start, C), :] = hfull
        return h

    lax.fori_loop(0, n_ctx + n_lat, fwd_body, zero_h, unroll=LRU_UNROLL)

    def bwd_body(c, h):
        start = pl.multiple_of(c * C, C)
        u = u_s[pl.ds(start, C), :]
        a, b = coeffs(u, 1)
        hfull, h = scan_chunk(a, b, h, True)
        y = hf_s[pl.ds(start, C), :] + hfull
        o_ref[0, pl.ds(start, C), :] = (y * _silu(ga_ref[0, pl.ds(start, C), :].astype(F32))).astype(o_ref.dtype)
        return h

    h = lax.fori_loop(0, n_ctx, lambda i, h: bwd_body(n_ctx - 1 - i, h), zero_h, unroll=LRU_UNROLL)
    lax.fori_loop(0, n_lat, lambda i, h: bwd_body(n_ctx + n_lat - 1 - i, h), h, unroll=LRU_UNROLL)


def _lru_branch(p3, conv_w, conv_b, w_cat, ba, bx, lam, n_ctx_rows):
    B, S, _ = p3.shape
    C, W, NB = LRU_CHUNK, LRU_BLOCK_W, LRU_NB
    WB = NB * W
    n_ctx, n_lat = n_ctx_rows // C, (S - n_ctx_rows) // C
    xa0, ga0 = COL_XA // WB, COL_GA // WB
    per_block = lambda rows: pl.BlockSpec((rows, WB), lambda b, n: (0, n))
    return pl.pallas_call(
        functools.partial(_lru_kernel, n_ctx=n_ctx, n_lat=n_lat),
        grid=(B, LRU_BLOCKS // NB),
        in_specs=[pl.BlockSpec((1, S, WB), lambda b, n: (b, 0, xa0 + n)),
                  pl.BlockSpec((1, S, WB), lambda b, n: (b, 0, ga0 + n)),
                  per_block(CONV_W), per_block(1),
                  pl.BlockSpec((2, NB, W, 2 * W), lambda b, n: (0, n, 0, 0)),
                  per_block(2), per_block(2), per_block(2)],
        out_specs=pl.BlockSpec((1, S, WB), lambda b, n: (b, 0, n)),
        out_shape=jax.ShapeDtypeStruct((B, S, D_MODEL), MXU_DT),
        scratch_shapes=[pltpu.VMEM((S, WB), F32), pltpu.VMEM((S, WB), F32),
                        pltpu.VMEM((C + 2 * LRU_HALO, WB), F32), pltpu.VMEM((NB, C, W), F32), pltpu.VMEM((NB, C, W), F32),
                        pltpu.VMEM((NB, C // SCAN_SEGS, SCAN_SEGS, W), F32),
                        pltpu.VMEM((NB, C // SCAN_SEGS, SCAN_SEGS, W), F32),
                        pltpu.VMEM((NB, C, W), F32), pltpu.VMEM((NB, SCAN_SEGS, W), F32)],
        compiler_params=_params(("arbitrary", "arbitrary")),
        name="lru",
    )(p3, p3, conv_w, conv_b.reshape(1, D_MODEL), w_cat, ba, bx, lam)


def _swap_quarters(x, q):
    lane = lax.broadcasted_iota(jnp.int32, (x.shape[0], 128), 1)
    cols = []
    for c in range(x.shape[1] // 128):
        xc = x[:, c * 128:(c + 1) * 128]
        if 2 * q == 128:
            cols.append(pltpu.roll(xc, q, 1))
        else:
            cols.append(jnp.where(lane % (2 * q) < q, pltpu.roll(xc, 128 - q, 1), pltpu.roll(xc, q, 1)))
    return cols[0] if len(cols) == 1 else jnp.concatenate(cols, axis=1)


RET_SUB = 2


def _ret_block_index(d, c, n_ctx, n_all):
    rev = jnp.where(c < n_ctx, n_ctx - 1 - c, n_ctx + n_all - 1 - c)
    return jnp.where(d == 0, c, rev)


def _ret_kernel(q_ref, k_ref, v_ref, g_ref, cos_ref, sin_ref, theta_ref, gain_ref,
                o_ref, s_s, of_s, oc_s, inner_s, dec_s, *, n_ctx, n_all):
    C, dh = RET_CHUNK, RET_HEAD_DIM
    RB = RET_SUB * C
    d = pl.program_id(1)
    c = pl.program_id(2)
    start = pl.multiple_of(_ret_block_index(d, c, n_ctx, n_all) * RB, RB)
    fwd = d == 0
    wide = lambda t: jnp.concatenate([t] * (dh // C), axis=1)

    @pl.when(c == 0)
    def _():
        s_s[...] = jnp.zeros_like(s_s)
        row = lax.broadcasted_iota(jnp.int32, (C, C), 0)
        col = lax.broadcasted_iota(jnp.int32, (C, C), 1)
        diff = jnp.where(fwd, row - col, col - row).astype(F32)
        q_pow = jnp.where(fwd, row + 1, C - row).astype(F32)
        k_pow = jnp.where(fwd, C - 1 - row, row).astype(F32)
        for h in range(RET_HEADS):
            lg = -_softplus(-theta_ref[pl.ds(d * RET_HEADS + h, 1), :])
            inner_s[h] = jnp.where(diff >= 0, jnp.exp(lg * jnp.maximum(diff, 0.0)), 0.0)
            dec_s[h, 0] = jnp.exp(lg * q_pow)
            dec_s[h, 1] = jnp.exp(lg * k_pow)
            dec_s[h, 2] = jnp.broadcast_to(jnp.exp(lg * float(C)), (C, C))

    k_scale = dh ** -0.5
    for t in range(RET_SUB):
        r0 = pl.multiple_of(jnp.where(fwd, t, RET_SUB - 1 - t) * C, C)
        rows = pl.ds(r0, C)
        cos = cos_ref[rows, :]
        sin = sin_ref[rows, :]
        for h in range(RET_HEADS):
            cs = slice(h * dh, (h + 1) * dh)
            q = q_ref[0, rows, cs].astype(F32)
            k = k_ref[0, rows, cs].astype(F32)
            q = q * cos + _swap_quarters(q, dh // 4) * sin
            k = (k * cos + _swap_quarters(k, dh // 4) * sin) * k_scale
            qm = q.astype(MXU_DT)
            km = k.astype(MXU_DT)
            vm = v_ref[0, rows, cs].astype(MXU_DT)
            s_old = s_s[h]
            att = lax.dot_general(qm, km, (((1,), (1,)), ((), ())), preferred_element_type=F32) * inner_s[h]
            oc_s[rows, cs] = (jnp.dot(att.astype(MXU_DT), vm, preferred_element_type=F32)
                              + jnp.dot(qm, s_old.astype(MXU_DT), preferred_element_type=F32) * wide(dec_s[h, 0]))
            kt = jnp.transpose(k * wide(dec_s[h, 1])).astype(MXU_DT)
            s_s[h] = s_old * wide(dec_s[h, 2, 0:1, :]) + jnp.dot(kt, vm, preferred_element_type=F32)

    @pl.when(fwd)
    def _():
        of_s[pl.ds(start, RB), :] = oc_s[...]

    @pl.when(jnp.logical_not(fwd))
    def _():
        for h in range(RET_HEADS):
            cs = slice(h * dh, (h + 1) * dh)
            t = oc_s[:, cs] + of_s[pl.ds(start, RB), cs]
            tc = t - jnp.mean(t, axis=-1, keepdims=True)
            y = tc * lax.rsqrt(jnp.mean(tc * tc, axis=-1, keepdims=True) + NORM_EPS) * gain_ref[:, cs]
            o_ref[0, :, cs] = (y * _silu(g_ref[0, :, cs].astype(F32))).astype(o_ref.dtype)


def _ret_branch(p3, cos_t, sin_t, theta, gain, n_ctx_rows):
    B, S, _ = p3.shape
    C = RET_CHUNK
    RB = RET_SUB * C
    assert n_ctx_rows % RB == 0 and S % RB == 0
    n_ctx, n_all = n_ctx_rows // RB, S // RB
    bidx = functools.partial(_ret_block_index, n_ctx=n_ctx, n_all=n_all)

    def pcol(col):
        return pl.BlockSpec((1, RB, D_MODEL), lambda b, d, c: (b, bidx(d, c), col // D_MODEL))

    def out_map(b, d, c):
        return (b, jnp.where(d == 0, bidx(1, 0), bidx(d, c)), 0)

    return pl.pallas_call(
        functools.partial(_ret_kernel, n_ctx=n_ctx, n_all=n_all),
        grid=(B, 2, n_all),
        in_specs=[pcol(COL_QB), pcol(COL_KB), pcol(COL_VB), pcol(COL_GB),
                  pl.BlockSpec((RB, RET_HEAD_DIM), lambda b, d, c: (bidx(d, c), 0)),
                  pl.BlockSpec((RB, RET_HEAD_DIM), lambda b, d, c: (bidx(d, c), 0)),
                  pl.BlockSpec((2 * RET_HEADS, C), lambda b, d, c: (0, 0)),
                  pl.BlockSpec((1, D_MODEL), lambda b, d, c: (0, 0))],
        out_specs=pl.BlockSpec((1, RB, D_MODEL), out_map),
        out_shape=jax.ShapeDtypeStruct((B, S, D_MODEL), MXU_DT),
        scratch_shapes=[pltpu.VMEM((RET_HEADS, RET_HEAD_DIM, RET_HEAD_DIM), F32),
                        pltpu.VMEM((S, D_MODEL), F32), pltpu.VMEM((RB, D_MODEL), F32),
                        pltpu.VMEM((RET_HEADS, C, C), F32), pltpu.VMEM((RET_HEADS, 3, C, C), F32)],
        compiler_params=_params(("arbitrary", "arbitrary", "arbitrary")),
        name="retention",
    )(p3, p3, p3, p3, cos_t, sin_t, jnp.broadcast_to(theta.reshape(2 * RET_HEADS, 1), (2 * RET_HEADS, C)),
      gain.reshape(1, D_MODEL))


def _rms_rows(x, gain, width):
    return x * lax.rsqrt(jnp.sum(x * x, axis=-1, keepdims=True) * (1.0 / width) + NORM_EPS) * gain


def _mla_prep_kernel(p_ref, wq_ref, wkv_ref, qn_ref, kvn_ref, gqn_ref, gqr_ref, gkn_ref, gkr_ref,
                     cos_ref, sin_ref, q_ref, k_ref, v_ref):
    H = MLA_HEADS
    blk = p_ref[0].astype(F32)
    qd = blk[:, 0:Q_LORA]
    kvd = blk[:, Q_LORA:Q_LORA + KV_LORA]
    kr = blk[:, Q_LORA + KV_LORA:Q_LORA + KV_LORA + 128]
    tm = blk.shape[0]
    lane = lax.broadcasted_iota(jnp.int32, (tm, 128), 1)
    low = lane < MLA_ROPE
    cos = cos_ref[...]
    sin = sin_ref[...]

    q = jnp.dot(_rms_rows(qd, qn_ref[...], Q_LORA).astype(MXU_DT), wq_ref[...], preferred_element_type=F32)
    kv = jnp.dot(_rms_rows(kvd, kvn_ref[...], KV_LORA).astype(MXU_DT), wkv_ref[...], preferred_element_type=F32)

    k_rope = _rms_rows(kr, gkr_ref[...], MLA_ROPE)
    k_rope = k_rope * cos[:, 0:128] + _swap_quarters(k_rope, MLA_ROPE // 4) * sin[:, 0:128]
    k_rope = jnp.where(low, k_rope, 0.0)

    for j in range(H // 2):
        xr = q[:, H * MLA_NOPE + j * 128:H * MLA_NOPE + (j + 1) * 128]
        sq = xr * xr
        ms_lo = jnp.sum(jnp.where(low, sq, 0.0), axis=-1, keepdims=True)
        ms_hi = jnp.sum(jnp.where(low, 0.0, sq), axis=-1, keepdims=True)
        inv = lax.rsqrt(jnp.where(low, ms_lo, ms_hi) * (1.0 / MLA_ROPE) + NORM_EPS)
        xr = xr * inv * gqr_ref[...]
        xr = xr * cos[:, 0:128] + _swap_quarters(xr, MLA_ROPE // 4) * sin[:, 0:128]
        xr_t = jnp.transpose(xr * Q_SCALE).astype(q_ref.dtype)
        for e in range(2):
            h = 2 * j + e
            qn = _rms_rows(q[:, h * MLA_NOPE:(h + 1) * MLA_NOPE], gqn_ref[...], MLA_NOPE) * Q_SCALE
            q_ref[0, h, 0:MLA_NOPE, :] = jnp.transpose(qn).astype(q_ref.dtype)
            q_ref[0, h, MLA_NOPE:MLA_NOPE + MLA_ROPE, :] = xr_t[e * MLA_ROPE:(e + 1) * MLA_ROPE, :]
            q_ref[0, h, MLA_NOPE + MLA_ROPE:, :] = jnp.zeros((MLA_QK_PAD - MLA_NOPE - MLA_ROPE, tm), q_ref.dtype)

    ones_row = jnp.where(lax.broadcasted_iota(jnp.int32, (MLA_VT_PAD - MLA_V, tm), 0) == 0, 1.0, 0.0)
    for h in range(H):
        kn = _rms_rows(kv[:, h * MLA_NOPE:(h + 1) * MLA_NOPE], gkn_ref[...], MLA_NOPE)
        k_ref[0, h, :, 0:MLA_NOPE] = kn.astype(k_ref.dtype)
        k_ref[0, h, :, MLA_NOPE:] = k_rope.astype(k_ref.dtype)
        v = kv[:, H * MLA_NOPE + h * MLA_V:H * MLA_NOPE + (h + 1) * MLA_V]
        v_ref[0, h, 0:MLA_V, :] = jnp.transpose(v).astype(v_ref.dtype)
        v_ref[0, h, MLA_V:, :] = ones_row.astype(v_ref.dtype)


def _mla_prep(p3, wq, wkv, q_norm, kv_norm, g_qn, g_qr2, g_kn, g_kr2, cos_t, sin_t):
    B, S, _ = p3.shape
    H = MLA_HEADS
    tm = 256
    full = lambda shape: pl.BlockSpec(shape, lambda b, i: (0,) * len(shape))
    return pl.pallas_call(
        _mla_prep_kernel,
        grid=(B, S // tm),
        in_specs=[pl.BlockSpec((1, tm, 1024), lambda b, i: (b, i, COL_TAIL // 1024)),
                  full(wq.shape), full(wkv.shape),
                  full((1, Q_LORA)), full((1, KV_LORA)), full((1, MLA_NOPE)), full((1, 128)),
                  full((1, MLA_NOPE)), full((1, 128)),
                  pl.BlockSpec((tm, 128), lambda b, i: (i, 0)),
                  pl.BlockSpec((tm, 128), lambda b, i: (i, 0))],
        out_specs=[pl.BlockSpec((1, H, MLA_QK_PAD, tm), lambda b, i: (b, 0, 0, i)),
                   pl.BlockSpec((1, H, tm, MLA_QK_PAD), lambda b, i: (b, 0, i, 0)),
                   pl.BlockSpec((1, H, MLA_VT_PAD, tm), lambda b, i: (b, 0, 0, i))],
        out_shape=[jax.ShapeDtypeStruct((B, H, MLA_QK_PAD, S), MXU_DT),
                   jax.ShapeDtypeStruct((B, H, S, MLA_QK_PAD), MXU_DT),
                   jax.ShapeDtypeStruct((B, H, MLA_VT_PAD, S), MXU_DT)],
        compiler_params=_params(("arbitrary", "arbitrary")),
        name="mla_prep",
    )(p3, wq, wkv, q_norm.reshape(1, -1), kv_norm.reshape(1, -1), g_qn.reshape(1, -1), g_qr2,
      g_kn.reshape(1, -1), g_kr2, cos_t, sin_t)


ATT_TQ = 256
ATT_TK = 256
ATT_HP = 4


def _attn_kernel(qt_ref, qtn_ref, k_ref, vt_ref, g_ref, o_ref, s_s, m_s, *, n_first, n_all_tiles):
    tq, tk = ATT_TQ, ATT_TK
    i = pl.program_id(2)
    heads = range(ATT_HP)

    def scores(qt, h, j):
        s = jnp.dot(k_ref[0, h, j * tk:(j + 1) * tk, :], qt, preferred_element_type=F32)
        s_s[h, j * tk:(j + 1) * tk, :] = s
        mm = s[0:8, :]
        for r in range(1, tk // 8):
            mm = jnp.maximum(mm, s[r * 8:(r + 1) * 8, :])
        return mm

    def set_query_max(mm, h):
        m_s[h] = jnp.broadcast_to(jnp.max(mm, axis=0, keepdims=True), (8, tq))

    def stage(n_kv):
        for h in heads:
            qt = qt_ref[0, h]
            mm = scores(qt, h, 0)
            for j in range(1, n_kv):
                mm = jnp.maximum(mm, scores(qt, h, j))
            set_query_max(mm, h)

    def attend(n_cur, n_next):
        m = [m_s[h, 0:1, :] for h in heads]
        qtn = [qtn_ref[0, h] for h in heads]
        acc = [jnp.zeros((MLA_VT_PAD, tq), F32) for _ in heads]
        mm = [None for _ in heads]
        for j in range(max(n_cur, n_next)):
            for h in heads:
                if j < n_cur:
                    p = jnp.exp2(s_s[h, j * tk:(j + 1) * tk, :] - m[h]).astype(MXU_DT)
                    acc[h] = acc[h] + jnp.dot(vt_ref[0, h, :, j * tk:(j + 1) * tk], p,
                                              preferred_element_type=F32)
                if j < n_next:
                    c = scores(qtn[h], h, j)
                    mm[h] = c if mm[h] is None else jnp.maximum(mm[h], c)
        for h in heads:
            set_query_max(mm[h], h)
            out_t = acc[h][0:MLA_V, :] / acc[h][MLA_V:MLA_V + 1, :]
            cs = slice(h * MLA_V, (h + 1) * MLA_V)
            o_ref[0, :, cs] = (jnp.transpose(out_t) * _silu(g_ref[0, :, cs].astype(F32))).astype(o_ref.dtype)

    pl.when(i == 0)(lambda: stage(n_first))
    if n_first == n_all_tiles:
        attend(n_all_tiles, n_all_tiles)
    else:
        pl.when(i == 0)(lambda: attend(n_first, n_all_tiles))
        pl.when(i > 0)(lambda: attend(n_all_tiles, n_all_tiles))


def _mla_attention(qt, kc, vt, p3, n_ctx_rows, with_ctx):
    B, H, S, _ = kc.shape
    tq, HP = ATT_TQ, ATT_HP
    assert n_ctx_rows == tq and ATT_TK == tq and H % HP == 0
    n_all_tiles = S // ATT_TK
    q_off = 0 if with_ctx else 1
    n_q = S // tq - q_off
    n_first = n_ctx_rows // ATT_TK if with_ctx else n_all_tiles
    gc0 = COL_GC // (HP * MLA_V)
    return pl.pallas_call(
        functools.partial(_attn_kernel, n_first=n_first, n_all_tiles=n_all_tiles),
        grid=(B, H // HP, n_q),
        in_specs=[pl.BlockSpec((1, HP, MLA_QK_PAD, tq), lambda b, h, i: (b, h, 0, i + q_off)),
                  pl.BlockSpec((1, HP, MLA_QK_PAD, tq),
                               lambda b, h, i: (b, h, 0, jnp.minimum(i + 1, n_q - 1) + q_off)),
                  pl.BlockSpec((1, HP, S, MLA_QK_PAD), lambda b, h, i: (b, h, 0, 0)),
                  pl.BlockSpec((1, HP, MLA_VT_PAD, S), lambda b, h, i: (b, h, 0, 0)),
                  pl.BlockSpec((1, tq, HP * MLA_V), lambda b, h, i: (b, i + q_off, gc0 + h))],
        out_specs=pl.BlockSpec((1, tq, HP * MLA_V), lambda b, h, i: (b, i, h)),
        out_shape=jax.ShapeDtypeStruct((B, n_q * tq, H * MLA_V), MXU_DT),
        scratch_shapes=[pltpu.VMEM((HP, S, tq), F32), pltpu.VMEM((HP, 8, tq), F32)],
        compiler_params=_params(("arbitrary", "arbitrary", "arbitrary")),
        name="mla_attention",
    )(qt, qt, kc, vt, p3)


def _merge_kernel(ya_ref, yb_ref, yc_ref, ma_ref, mb_ref, mc_ref, x_ref, mod_ref,
                  wa_ref, wb_ref, wc_ref, wo_ref, o_ref):
    def down(y_ref, m_ref, w_ref):
        return jax.nn.sigmoid(m_ref[0].astype(F32)) * jnp.dot(y_ref[0], w_ref[...], preferred_element_type=F32)

    z = down(ya_ref, ma_ref, wa_ref) + down(yb_ref, mb_ref, wb_ref) + down(yc_ref, mc_ref, wc_ref)
    out = jnp.dot(z.astype(MXU_DT), wo_ref[...], preferred_element_type=F32)
    o_ref[0] = x_ref[0] + mod_ref[0, 2:3] * out


def _merge(ya, yb, yc, p3, xs, modrows, wa, wb, wc, wo, row_off):
    B, S, _ = xs.shape
    tm = SUBTILE
    o = row_off // tm
    n_sub = S // tm
    act = pl.BlockSpec((1, tm, D_MODEL), lambda b, i: (b, i + o, 0))
    o_c = o - (S - yc.shape[1]) // tm
    act_c = pl.BlockSpec((1, tm, D_MODEL), lambda b, i: (b, i + o_c, 0))
    gm = lambda k: pl.BlockSpec((1, tm, D_MODEL), lambda b, i: (b, i + o, COL_GM // D_MODEL + k))
    wspec = pl.BlockSpec((D_MODEL, D_MODEL), lambda b, i: (0, 0))
    return pl.pallas_call(
        _merge_kernel,
        grid=(B, n_sub - o),
        in_specs=[act, act, act_c, gm(0), gm(1), gm(2), act,
                  pl.BlockSpec((1, 8, D_MODEL), lambda b, i: (b * n_sub + i + o, 0, 0)),
                  wspec, wspec, wspec, wspec],
        out_specs=pl.BlockSpec((1, tm, D_MODEL), lambda b, i: (b, i, 0)),
        out_shape=jax.ShapeDtypeStruct((B, S - row_off, D_MODEL), F32),
        compiler_params=_params(("arbitrary", "arbitrary")),
        name="merge",
    )(ya, yb, yc, p3, p3, p3, xs, modrows, wa, wb, wc, wo)


def _rope_tables(n_ctx_rows, n_lat_rows, dim):
    rows = n_lat_rows // GRID_W
    row = np.repeat(np.arange(rows), GRID_W).astype(np.float64)
    col = np.tile(np.arange(GRID_W), rows).astype(np.float64)
    n_freq = dim // 4
    inv = ROPE_BASE ** (-np.arange(n_freq, dtype=np.float64) / n_freq)
    ang_r = row[:, None] * inv[None, :]
    ang_c = col[:, None] * inv[None, :]
    ang = np.concatenate([ang_r, ang_r, ang_c, ang_c], axis=-1)
    sign = np.tile(np.concatenate([-np.ones(n_freq), np.ones(n_freq)]), 2)
    cos = np.concatenate([np.ones((n_ctx_rows, dim)), np.cos(ang)], axis=0)
    sin = np.concatenate([np.zeros((n_ctx_rows, dim)), np.sin(ang) * sign], axis=0)
    return cos.astype(np.float32), sin.astype(np.float32)


def _permute_w_in(w):
    xa_to_gb = w[..., :6144]
    tail = w[..., 6144:6848]
    gc_gm = w[..., 6848:]
    pad = jnp.zeros(w.shape[:-1] + (P_COLS - w.shape[-1],), w.dtype)
    return jnp.concatenate([xa_to_gb, gc_gm, tail, pad], axis=-1).astype(MXU_DT)


def _split_heads(w, first):
    K = w.shape[0]
    w3 = w.reshape(K, MLA_HEADS, -1)
    return jnp.concatenate([w3[:, :, :first].reshape(K, -1), w3[:, :, first:].reshape(K, -1)],
                           axis=1).astype(MXU_DT)


def _pad_lanes(g):
    return jnp.concatenate([g, jnp.zeros((128 - g.shape[0],), g.dtype)]).reshape(1, 128)


def kernel(x, c, ctx, c_ctx, w_mod, b_mod, norm_gain, w_in, conv_w, conv_b, lru_wa, lru_ba, lru_wx, lru_bx,
           lru_lambda, ret_theta, ret_gain, mla_q_norm, mla_w_q_up, mla_kv_norm, mla_w_kv_up,
           mla_g_qn, mla_g_qr, mla_g_kn, mla_g_kr, w_down_a, w_down_b, w_down_c, w_out):
    B, T, D = x.shape
    L = ctx.shape[1]
    S = L + T
    depth = w_in.shape[0]
    assert D == D_MODEL and L % SUBTILE == 0 and T % SUBTILE == 0 and T % GRID_W == 0

    cos_r, sin_r = _rope_tables(L, T, RET_HEAD_DIM)
    cos_m, sin_m = _rope_tables(L, T, MLA_ROPE)
    pad64 = lambda t: np.concatenate([t, t], axis=1)
    cos_m, sin_m = pad64(cos_m), pad64(sin_m)

    n_rows = ((B + 1 + 7) // 8) * 8
    cc = jnp.zeros((n_rows, D), F32).at[:B].set(c).at[B].set(c_ctx)
    n_sub = S // SUBTILE
    sub = jnp.arange(B * n_sub)
    mod_row = jnp.where(sub % n_sub < L // SUBTILE, B, sub // n_sub)

    xs = jnp.concatenate([ctx, x], axis=1)
    for l in range(depth):
        last = l == depth - 1
        mod = _modulation(cc, w_mod, b_mod, l)
        mod3 = mod.reshape(n_rows, 3, D)[mod_row]
        modrows = jnp.concatenate([mod3, jnp.zeros((B * n_sub, 5, D), F32)], axis=1)

        p3 = _inproj(xs.reshape(B * S, D), modrows, norm_gain[l], _permute_w_in(w_in[l])).reshape(B, S, P_COLS)

        w_cat = jnp.concatenate([lru_wa[l], lru_wx[l]], axis=-1).astype(MXU_DT)
        ya = _lru_branch(p3, conv_w[l], conv_b[l], w_cat, lru_ba[l], lru_bx[l], lru_lambda[l], L)
        yb = _ret_branch(p3, cos_r, sin_r, ret_theta[l], ret_gain[l], L)
        qc, kc, vc = _mla_prep(p3, _split_heads(mla_w_q_up[l], MLA_NOPE), _split_heads(mla_w_kv_up[l], MLA_NOPE),
                               mla_q_norm[l], mla_kv_norm[l], mla_g_qn[l],
                               jnp.concatenate([mla_g_qr[l], mla_g_qr[l]]).reshape(1, 128),
                               mla_g_kn[l], _pad_lanes(mla_g_kr[l]), cos_m, sin_m)
        yc = _mla_attention(qc, kc, vc, p3, L, not last)
        xs = _merge(ya, yb, yc, p3, xs, modrows,
                    w_down_a[l].astype(MXU_DT), w_down_b[l].astype(MXU_DT), w_down_c[l].astype(MXU_DT),
                    w_out[l].astype(MXU_DT), L if last else 0)
    return xs
```

```python
import functools
import math

import jax
import jax.numpy as jnp
import numpy as np
from jax import lax
from jax.experimental import pallas as pl
from jax.experimental.pallas import tpu as pltpu

D_MODEL = 1024
GRID_W = 64
ROPE_BASE = 10000.0
NORM_EPS = 1e-6

LRU_BLOCKS = 8
LRU_BLOCK_W = D_MODEL // LRU_BLOCKS
CONV_W = 4
LRU_C = 8.0

RET_HEADS = 4
RET_HEAD_DIM = D_MODEL // RET_HEADS
RET_CHUNK = 128

MLA_HEADS = 8
MLA_NOPE = 128
MLA_ROPE = 64
MLA_V = 128
Q_LORA = 384
KV_LORA = 256
MLA_SCALE = (MLA_NOPE + MLA_ROPE) ** -0.5
Q_SCALE = MLA_SCALE * math.log2(math.e)
MLA_VT_PAD = 144
MLA_QK_PAD = 256

P_COLS = 11 * 1024
COL_XA, COL_GA, COL_QB, COL_KB, COL_VB, COL_GB, COL_GC, COL_GM, COL_TAIL = (
    0, 1024, 2048, 3072, 4096, 5120, 6144, 7168, 10240)

MXU_DT = jnp.bfloat16
P_DT = jnp.bfloat16
SUBTILE = 256
VMEM_LIMIT = 56 * 1024 * 1024
F32 = jnp.float32


def _params(sem):
    return pltpu.CompilerParams(dimension_semantics=sem, vmem_limit_bytes=VMEM_LIMIT)


def _silu(x):
    return x * jax.nn.sigmoid(x)


def _softplus(x):
    return jnp.maximum(x, 0.0) + jnp.log1p(jnp.exp(-jnp.abs(x)))


def _pick(n, cands):
    for c in cands:
        if n % c == 0:
            return c
    raise ValueError(f"no tile in {cands} divides {n}")


def _mod_kernel(c_ref, w_ref, b_ref, o_ref):
    a = _silu(c_ref[...])
    o_ref[...] = jnp.dot(a, w_ref[...], preferred_element_type=F32) + b_ref[...]


def _modulation(cc, w_mod, b_mod, layer):
    rows = cc.shape[0]
    depth, _, n3 = w_mod.shape
    tn = 1024
    return pl.pallas_call(
        _mod_kernel,
        grid=(n3 // tn,),
        in_specs=[pl.BlockSpec((rows, D_MODEL), lambda j: (0, 0)),
                  pl.BlockSpec((None, D_MODEL, tn), lambda j: (layer, 0, j)),
                  pl.BlockSpec((None, 1, tn), lambda j: (layer, 0, j))],
        out_specs=pl.BlockSpec((rows, tn), lambda j: (0, j)),
        out_shape=jax.ShapeDtypeStruct((rows, n3), F32),
        compiler_params=_params(("arbitrary",)),
        name="modulation",
    )(cc, w_mod, b_mod.reshape(depth, 1, n3))


def _inproj_kernel(*refs, tm, n_sub, n_ctx_sub, split):
    nsub = tm // SUBTILE
    if split:
        ctx_refs, x_refs = refs[:nsub], refs[nsub:2 * nsub]
        mod_ref, gain_ref, w_ref, o_ref, h_ref = refs[2 * nsub:]
    else:
        x_ref, mod_ref, gain_ref, w_ref, o_ref, h_ref = refs

    @pl.when(pl.program_id(1) == 0)
    def _():
        gain = gain_ref[...]
        for s in range(nsub):
            rows = slice(s * SUBTILE, (s + 1) * SUBTILE)
            if split:
                r = (pl.program_id(0) * nsub + s) % n_sub
                x = jnp.where(r < n_ctx_sub, ctx_refs[s][0], x_refs[s][0])
            else:
                x = x_ref[rows, :]
            y = x * lax.rsqrt(jnp.mean(x * x, axis=-1, keepdims=True) + NORM_EPS) * gain
            h_ref[rows, :] = (y * (1.0 + mod_ref[s, 1:2, :]) + mod_ref[s, 0:1, :]).astype(h_ref.dtype)

    o_ref[...] = jnp.dot(h_ref[...], w_ref[...], preferred_element_type=F32).astype(o_ref.dtype)


def _inproj(srcs, modrows, gain, w_in_p):
    split = len(srcs) == 2
    B = srcs[0].shape[0]
    S = sum(a.shape[1] for a in srcs)
    n_sub, n_ctx_sub = S // SUBTILE, srcs[0].shape[1] // SUBTILE
    n = B * S
    tm = _pick(n, (1024, 512, 256))
    nsub = tm // SUBTILE
    tn = P_COLS // 4
    if split:
        def ctx_map(s):
            return lambda i, j: ((i * nsub + s) // n_sub, jnp.minimum((i * nsub + s) % n_sub, n_ctx_sub - 1), 0)

        def x_map(s):
            return lambda i, j: ((i * nsub + s) // n_sub, jnp.maximum((i * nsub + s) % n_sub - n_ctx_sub, 0), 0)

        blk = (1, SUBTILE, D_MODEL)
        src_specs = ([pl.BlockSpec(blk, ctx_map(s)) for s in range(nsub)]
                     + [pl.BlockSpec(blk, x_map(s)) for s in range(nsub)])
        src_args = [srcs[0]] * nsub + [srcs[1]] * nsub
    else:
        src_specs = [pl.BlockSpec((tm, D_MODEL), lambda i, j: (i, 0))]
        src_args = [srcs[0].reshape(n, D_MODEL)]
    return pl.pallas_call(
        functools.partial(_inproj_kernel, tm=tm, n_sub=n_sub, n_ctx_sub=n_ctx_sub, split=split),
        grid=(n // tm, P_COLS // tn),
        in_specs=src_specs + [pl.BlockSpec((nsub, 8, D_MODEL), lambda i, j: (i, 0, 0)),
                              pl.BlockSpec((1, D_MODEL), lambda i, j: (0, 0)),
                              pl.BlockSpec((D_MODEL, tn), lambda i, j: (0, j))],
        out_specs=pl.BlockSpec((tm, tn), lambda i, j: (i, j)),
        out_shape=jax.ShapeDtypeStruct((n, P_COLS), P_DT),
        scratch_shapes=[pltpu.VMEM((tm, D_MODEL), MXU_DT)],
        compiler_params=_params(("arbitrary", "arbitrary")),
        name="inproj",
    )(*src_args, modrows, gain.reshape(1, D_MODEL), w_in_p).reshape(B, S, P_COLS)


LRU_CHUNK = 64
LRU_NB = 4
SCAN_SEGS = 8
LRU_HALO = 16
LRU_UNROLL = 4


def _lru_kernel(xa_ref, ga_ref, cw_ref, cb_ref, w_ref, ba_ref, bx_ref, lam_ref,
                o_ref, u_s, hf_s, xe_s, as_s, bs_s, hs_s, ac_s, ho_s, e_s, *, n_ctx, n_lat):
    C = LRU_CHUNK
    W = LRU_BLOCK_W
    WB = LRU_NB * W
    SEG = C // SCAN_SEGS
    S = (n_ctx + n_lat) * C
    cw = cw_ref[...]
    cb = cb_ref[...]
    sub = lax.broadcasted_iota(jnp.int32, (SCAN_SEGS, W), 0)

    def conv_chunk(start, first, last):
        E = LRU_HALO
        x = xa_ref[0, pl.ds(start, C), :].astype(F32)
        prev = xa_ref[0, pl.ds(pl.multiple_of(jnp.maximum(start - E, 0), E), E), :].astype(F32)
        nxt = xa_ref[0, pl.ds(pl.multiple_of(jnp.minimum(start + C, S - E), E), E), :].astype(F32)
        xe_s[0:E, :] = jnp.where(first, 0.0, prev)
        xe_s[E:E + C, :] = x
        xe_s[E + C:, :] = jnp.where(last, 0.0, nxt)
        return (xe_s[E - 2:E - 2 + C, :] * cw[0:1] + xe_s[E - 1:E - 1 + C, :] * cw[1:2]
                + x * cw[2:3] + xe_s[E + 1:E + 1 + C, :] * cw[3:4] + cb)

    def coeffs(u, d):
        um = u.astype(MXU_DT)
        z = [jnp.dot(um[:, n * W:(n + 1) * W], w_ref[d, n], preferred_element_type=F32) for n in range(LRU_NB)]
        za = jnp.concatenate([zn[:, :W] for zn in z], axis=1)
        zx = jnp.concatenate([zn[:, W:] for zn in z], axis=1)
        r = jax.nn.sigmoid(za + ba_ref[d:d + 1, :])
        i = jax.nn.sigmoid(zx + bx_ref[d:d + 1, :])
        log_a = (-LRU_C * r) * _softplus(-lam_ref[d:d + 1, :])
        a = jnp.exp(log_a)
        g = -jnp.tanh(log_a) * (1.0 + a * a)
        root = jnp.where(g > 0.0, g * lax.rsqrt(g), 0.0)
        return a, root * (i * u)

    def scan_block(n, h, reverse):
        hk = ak = None
        for k in (range(SEG - 1, -1, -1) if reverse else range(SEG)):
            a_k = as_s[n, pl.ds(k, SCAN_SEGS, stride=SEG), :]
            b_k = bs_s[n, pl.ds(k, SCAN_SEGS, stride=SEG), :]
            if hk is None:
                hk, ak = b_k, a_k
            else:
                hk = a_k * hk + b_k
                ak = a_k * ak
            hs_s[n, k] = hk
            ac_s[n, k] = ak
        s = 1
        while s < SCAN_SEGS:
            shift = SCAN_SEGS - s if reverse else s
            m = sub < SCAN_SEGS - s if reverse else sub >= s
            h_o = pltpu.roll(hk, shift, 0)
            a_o = pltpu.roll(ak, shift, 0)
            hk = jnp.where(m, ak * h_o + hk, hk)
            ak = jnp.where(m, ak * a_o, ak)
            s *= 2
        end = hk + ak * h
        e_s[n] = end
        if reverse:
            cin = jnp.where(sub == SCAN_SEGS - 1, h, pltpu.roll(end, SCAN_SEGS - 1, 0))
            h_new = e_s[n, 0:1, :]
        else:
            cin = jnp.where(sub == 0, h, pltpu.roll(end, 1, 0))
            h_new = e_s[n, SCAN_SEGS - 1:SCAN_SEGS, :]
        for k in range(SEG):
            ho_s[n, pl.ds(k, SCAN_SEGS, stride=SEG), :] = hs_s[n, k] + ac_s[n, k] * cin
        return h_new

    def scan_chunk(a, b, h, reverse):
        for n in range(LRU_NB):
            as_s[n] = a[:, n * W:(n + 1) * W]
            bs_s[n] = b[:, n * W:(n + 1) * W]
        h_new = [scan_block(n, h[:, n * W:(n + 1) * W], reverse) for n in range(LRU_NB)]
        return (jnp.concatenate([ho_s[n] for n in range(LRU_NB)], axis=1),
                jnp.concatenate(h_new, axis=1))

    zero_h = jnp.zeros((1, WB), F32)

    def fwd_body(c, h):
        start = pl.multiple_of(c * C, C)
        first = jnp.logical_or(c == 0, c == n_ctx)
        last = jnp.logical_or(c == n_ctx - 1, c == n_ctx + n_lat - 1)
        u = conv_chunk(start, first, last)
        u_s[pl.ds(start, C), :] = u
        a, b = coeffs(u, 0)
        hfull, h = scan_chunk(a, b, h, False)
        hf_s[pl.ds(start, C), :] = hfull
        return h

    lax.fori_loop(0, n_ctx + n_lat, fwd_body, zero_h, unroll=LRU_UNROLL)

    def bwd_body(c, h):
        start = pl.multiple_of(c * C, C)
        u = u_s[pl.ds(start, C), :]
        a, b = coeffs(u, 1)
        hfull, h = scan_chunk(a, b, h, True)
        y = hf_s[pl.ds(start, C), :] + hfull
        o_ref[0, pl.ds(start, C), :] = (y * _silu(ga_ref[0, pl.ds(start, C), :].astype(F32))).astype(o_ref.dtype)
        return h

    h = lax.fori_loop(0, n_ctx, lambda i, h: bwd_body(n_ctx - 1 - i, h), zero_h, unroll=LRU_UNROLL)
    lax.fori_loop(0, n_lat, lambda i, h: bwd_body(n_ctx + n_lat - 1 - i, h), h, unroll=LRU_UNROLL)


def _lru_branch(p3, conv_w, conv_b, w_cat, ba, bx, lam, n_ctx_rows):
    B, S, _ = p3.shape
    C, W, NB = LRU_CHUNK, LRU_BLOCK_W, LRU_NB
    WB = NB * W
    n_ctx, n_lat = n_ctx_rows // C, (S - n_ctx_rows) // C
    xa0, ga0 = COL_XA // WB, COL_GA // WB
    per_block = lambda rows: pl.BlockSpec((rows, WB), lambda b, n: (0, n))
    return pl.pallas_call(
        functools.partial(_lru_kernel, n_ctx=n_ctx, n_lat=n_lat),
        grid=(B, LRU_BLOCKS // NB),
        in_specs=[pl.BlockSpec((1, S, WB), lambda b, n: (b, 0, xa0 + n)),
                  pl.BlockSpec((1, S, WB), lambda b, n: (b, 0, ga0 + n)),
                  per_block(CONV_W), per_block(1),
                  pl.BlockSpec((2, NB, W, 2 * W), lambda b, n: (0, n, 0, 0)),
                  per_block(2), per_block(2), per_block(2)],
        out_specs=pl.BlockSpec((1, S, WB), lambda b, n: (b, 0, n)),
        out_shape=jax.ShapeDtypeStruct((B, S, D_MODEL), MXU_DT),
        scratch_shapes=[pltpu.VMEM((S, WB), F32), pltpu.VMEM((S, WB), F32),
                        pltpu.VMEM((C + 2 * LRU_HALO, WB), F32), pltpu.VMEM((NB, C, W), F32), pltpu.VMEM((NB, C, W), F32),
                        pltpu.VMEM((NB, C // SCAN_SEGS, SCAN_SEGS, W), F32),
                        pltpu.VMEM((NB, C // SCAN_SEGS, SCAN_SEGS, W), F32),
                        pltpu.VMEM((NB, C, W), F32), pltpu.VMEM((NB, SCAN_SEGS, W), F32)],
        compiler_params=_params(("arbitrary", "arbitrary")),
        name="lru",
    )(p3, p3, conv_w, conv_b.reshape(1, D_MODEL), w_cat, ba, bx, lam)


def _swap_quarters(x, q):
    lane = lax.broadcasted_iota(jnp.int32, (x.shape[0], 128), 1)
    cols = []
    for c in range(x.shape[1] // 128):
        xc = x[:, c * 128:(c + 1) * 128]
        if 2 * q == 128:
            cols.append(pltpu.roll(xc, q, 1))
        else:
            cols.append(jnp.where(lane % (2 * q) < q, pltpu.roll(xc, 128 - q, 1), pltpu.roll(xc, q, 1)))
    return cols[0] if len(cols) == 1 else jnp.concatenate(cols, axis=1)


RET_SUB = 2


def _ret_block_index(d, c, n_ctx, n_all):
    rev = jnp.where(c < n_ctx, n_ctx - 1 - c, n_ctx + n_all - 1 - c)
    return jnp.where(d == 0, c, rev)


def _ret_kernel(q_ref, k_ref, v_ref, g_ref, cos_ref, sin_ref, theta_ref, gain_ref,
                o_ref, s_s, of_s, oc_s, inner_s, dec_s, *, n_ctx, n_all):
    C, dh = RET_CHUNK, RET_HEAD_DIM
    RB = RET_SUB * C
    d = pl.program_id(1)
    c = pl.program_id(2)
    start = pl.multiple_of(_ret_block_index(d, c, n_ctx, n_all) * RB, RB)
    fwd = d == 0
    wide = lambda t: jnp.concatenate([t] * (dh // C), axis=1)

    @pl.when(c == 0)
    def _():
        s_s[...] = jnp.zeros_like(s_s)
        row = lax.broadcasted_iota(jnp.int32, (C, C), 0)
        col = lax.broadcasted_iota(jnp.int32, (C, C), 1)
        diff = jnp.where(fwd, row - col, col - row).astype(F32)
        q_pow = jnp.where(fwd, row + 1, C - row).astype(F32)
        k_pow = jnp.where(fwd, C - 1 - row, row).astype(F32)
        for h in range(RET_HEADS):
            lg = -_softplus(-theta_ref[pl.ds(d * RET_HEADS + h, 1), :])
            inner_s[h] = jnp.where(diff >= 0, jnp.exp(lg * jnp.maximum(diff, 0.0)), 0.0)
            dec_s[h, 0] = jnp.exp(lg * q_pow)
            dec_s[h, 1] = jnp.exp(lg * k_pow)
            dec_s[h, 2] = jnp.broadcast_to(jnp.exp(lg * float(C)), (C, C))

    k_scale = dh ** -0.5
    for t in range(RET_SUB):
        r0 = pl.multiple_of(jnp.where(fwd, t, RET_SUB - 1 - t) * C, C)
        rows = pl.ds(r0, C)
        cos = cos_ref[rows, :]
        sin = sin_ref[rows, :]
        for h in range(RET_HEADS):
            cs = slice(h * dh, (h + 1) * dh)
            q = q_ref[0, rows, cs].astype(F32)
            k = k_ref[0, rows, cs].astype(F32)
            q = q * cos + _swap_quarters(q, dh // 4) * sin
            k = (k * cos + _swap_quarters(k, dh // 4) * sin) * k_scale
            qm = q.astype(MXU_DT)
            km = k.astype(MXU_DT)
            vm = v_ref[0, rows, cs].astype(MXU_DT)
            s_old = s_s[h]
            att = lax.dot_general(qm, km, (((1,), (1,)), ((), ())), preferred_element_type=F32) * inner_s[h]
            oc_s[rows, cs] = (jnp.dot(att.astype(MXU_DT), vm, preferred_element_type=F32)
                              + jnp.dot(qm, s_old.astype(MXU_DT), preferred_element_type=F32) * wide(dec_s[h, 0]))
            kt = jnp.transpose(k * wide(dec_s[h, 1])).astype(MXU_DT)
            s_s[h] = s_old * wide(dec_s[h, 2, 0:1, :]) + jnp.dot(kt, vm, preferred_element_type=F32)

    @pl.when(fwd)
    def _():
        of_s[pl.ds(start, RB), :] = oc_s[...]

    @pl.when(jnp.logical_not(fwd))
    def _():
        for h in range(RET_HEADS):
            cs = slice(h * dh, (h + 1) * dh)
            t = oc_s[:, cs] + of_s[pl.ds(start, RB), cs]
            tc = t - jnp.mean(t, axis=-1, keepdims=True)
            y = tc * lax.rsqrt(jnp.mean(tc * tc, axis=-1, keepdims=True) + NORM_EPS) * gain_ref[:, cs]
            o_ref[0, :, cs] = (y * _silu(g_ref[0, :, cs].astype(F32))).astype(o_ref.dtype)


def _ret_branch(p3, cos_t, sin_t, theta, gain, n_ctx_rows):
    B, S, _ = p3.shape
    C = RET_CHUNK
    RB = RET_SUB * C
    assert n_ctx_rows % RB == 0 and S % RB == 0
    n_ctx, n_all = n_ctx_rows // RB, S // RB
    bidx = functools.partial(_ret_block_index, n_ctx=n_ctx, n_all=n_all)

    def pcol(col):
        return pl.BlockSpec((1, RB, D_MODEL), lambda b, d, c: (b, bidx(d, c), col // D_MODEL))

    def out_map(b, d, c):
        return (b, jnp.where(d == 0, bidx(1, 0), bidx(d, c)), 0)

    return pl.pallas_call(
        functools.partial(_ret_kernel, n_ctx=n_ctx, n_all=n_all),
        grid=(B, 2, n_all),
        in_specs=[pcol(COL_QB), pcol(COL_KB), pcol(COL_VB), pcol(COL_GB),
                  pl.BlockSpec((RB, RET_HEAD_DIM), lambda b, d, c: (bidx(d, c), 0)),
                  pl.BlockSpec((RB, RET_HEAD_DIM), lambda b, d, c: (bidx(d, c), 0)),
                  pl.BlockSpec((2 * RET_HEADS, C), lambda b, d, c: (0, 0)),
                  pl.BlockSpec((1, D_MODEL), lambda b, d, c: (0, 0))],
        out_specs=pl.BlockSpec((1, RB, D_MODEL), out_map),
        out_shape=jax.ShapeDtypeStruct((B, S, D_MODEL), MXU_DT),
        scratch_shapes=[pltpu.VMEM((RET_HEADS, RET_HEAD_DIM, RET_HEAD_DIM), F32),
                        pltpu.VMEM((S, D_MODEL), F32), pltpu.VMEM((RB, D_MODEL), F32),
                        pltpu.VMEM((RET_HEADS, C, C), F32), pltpu.VMEM((RET_HEADS, 3, C, C), F32)],
        compiler_params=_params(("arbitrary", "arbitrary", "arbitrary")),
        name="retention",
    )(p3, p3, p3, p3, cos_t, sin_t, jnp.broadcast_to(theta.reshape(2 * RET_HEADS, 1), (2 * RET_HEADS, C)),
      gain.reshape(1, D_MODEL))


def _rms_rows(x, gain, width):
    return x * lax.rsqrt(jnp.sum(x * x, axis=-1, keepdims=True) * (1.0 / width) + NORM_EPS) * gain


def _mla_prep_kernel(p_ref, wq_ref, wkv_ref, qn_ref, kvn_ref, gqn_ref, gqr_ref, gkn_ref, gkr_ref,
                     cos_ref, sin_ref, q_ref, k_ref, v_ref):
    H = MLA_HEADS
    blk = p_ref[0].astype(F32)
    qd = blk[:, 0:Q_LORA]
    kvd = blk[:, Q_LORA:Q_LORA + KV_LORA]
    kr = blk[:, Q_LORA + KV_LORA:Q_LORA + KV_LORA + 128]
    tm = blk.shape[0]
    lane = lax.broadcasted_iota(jnp.int32, (tm, 128), 1)
    low = lane < MLA_ROPE
    cos = cos_ref[...]
    sin = sin_ref[...]

    q = jnp.dot(_rms_rows(qd, qn_ref[...], Q_LORA).astype(MXU_DT), wq_ref[...], preferred_element_type=F32)
    kv = jnp.dot(_rms_rows(kvd, kvn_ref[...], KV_LORA).astype(MXU_DT), wkv_ref[...], preferred_element_type=F32)

    k_rope = _rms_rows(kr, gkr_ref[...], MLA_ROPE)
    k_rope = k_rope * cos[:, 0:128] + _swap_quarters(k_rope, MLA_ROPE // 4) * sin[:, 0:128]
    k_rope = jnp.where(low, k_rope, 0.0)

    for j in range(H // 2):
        xr = q[:, H * MLA_NOPE + j * 128:H * MLA_NOPE + (j + 1) * 128]
        sq = xr * xr
        ms_lo = jnp.sum(jnp.where(low, sq, 0.0), axis=-1, keepdims=True)
        ms_hi = jnp.sum(jnp.where(low, 0.0, sq), axis=-1, keepdims=True)
        inv = lax.rsqrt(jnp.where(low, ms_lo, ms_hi) * (1.0 / MLA_ROPE) + NORM_EPS)
        xr = xr * inv * gqr_ref[...]
        xr = xr * cos[:, 0:128] + _swap_quarters(xr, MLA_ROPE // 4) * sin[:, 0:128]
        xr_t = jnp.transpose(xr * Q_SCALE).astype(q_ref.dtype)
        for e in range(2):
            h = 2 * j + e
            qn = _rms_rows(q[:, h * MLA_NOPE:(h + 1) * MLA_NOPE], gqn_ref[...], MLA_NOPE) * Q_SCALE
            q_ref[0, h, 0:MLA_NOPE, :] = jnp.transpose(qn).astype(q_ref.dtype)
            q_ref[0, h, MLA_NOPE:MLA_NOPE + MLA_ROPE, :] = xr_t[e * MLA_ROPE:(e + 1) * MLA_ROPE, :]
            q_ref[0, h, MLA_NOPE + MLA_ROPE:, :] = jnp.zeros((MLA_QK_PAD - MLA_NOPE - MLA_ROPE, tm), q_ref.dtype)

    ones_row = jnp.where(lax.broadcasted_iota(jnp.int32, (MLA_VT_PAD - MLA_V, tm), 0) == 0, 1.0, 0.0)
    for h in range(H):
        kn = _rms_rows(kv[:, h * MLA_NOPE:(h + 1) * MLA_NOPE], gkn_ref[...], MLA_NOPE)
        k_ref[0, h, :, 0:MLA_NOPE] = kn.astype(k_ref.dtype)
        k_ref[0, h, :, MLA_NOPE:] = k_rope.astype(k_ref.dtype)
        v = kv[:, H * MLA_NOPE + h * MLA_V:H * MLA_NOPE + (h + 1) * MLA_V]
        v_ref[0, h, 0:MLA_V, :] = jnp.transpose(v).astype(v_ref.dtype)
        v_ref[0, h, MLA_V:, :] = ones_row.astype(v_ref.dtype)


def _mla_prep(p3, wq, wkv, q_norm, kv_norm, g_qn, g_qr2, g_kn, g_kr2, cos_t, sin_t):
    B, S, _ = p3.shape
    H = MLA_HEADS
    tm = 256
    full = lambda shape: pl.BlockSpec(shape, lambda b, i: (0,) * len(shape))
    return pl.pallas_call(
        _mla_prep_kernel,
        grid=(B, S // tm),
        in_specs=[pl.BlockSpec((1, tm, 1024), lambda b, i: (b, i, COL_TAIL // 1024)),
                  full(wq.shape), full(wkv.shape),
                  full((1, Q_LORA)), full((1, KV_LORA)), full((1, MLA_NOPE)), full((1, 128)),
                  full((1, MLA_NOPE)), full((1, 128)),
                  pl.BlockSpec((tm, 128), lambda b, i: (i, 0)),
                  pl.BlockSpec((tm, 128), lambda b, i: (i, 0))],
        out_specs=[pl.BlockSpec((1, H, MLA_QK_PAD, tm), lambda b, i: (b, 0, 0, i)),
                   pl.BlockSpec((1, H, tm, MLA_QK_PAD), lambda b, i: (b, 0, i, 0)),
                   pl.BlockSpec((1, H, MLA_VT_PAD, tm), lambda b, i: (b, 0, 0, i))],
        out_shape=[jax.ShapeDtypeStruct((B, H, MLA_QK_PAD, S), MXU_DT),
                   jax.ShapeDtypeStruct((B, H, S, MLA_QK_PAD), MXU_DT),
                   jax.ShapeDtypeStruct((B, H, MLA_VT_PAD, S), MXU_DT)],
        compiler_params=_params(("arbitrary", "arbitrary")),
        name="mla_prep",
    )(p3, wq, wkv, q_norm.reshape(1, -1), kv_norm.reshape(1, -1), g_qn.reshape(1, -1), g_qr2,
      g_kn.reshape(1, -1), g_kr2, cos_t, sin_t)


ATT_TQ = 256
ATT_TK = 256
ATT_HP = 4


def _attn_kernel(qt_ref, qtn_ref, k_ref, vt_ref, g_ref, o_ref, s_s, m_s, *, n_first, n_all_tiles):
    tq, tk = ATT_TQ, ATT_TK
    i = pl.program_id(2)
    heads = range(ATT_HP)

    def scores(qt, h, j):
        s = jnp.dot(k_ref[0, h, j * tk:(j + 1) * tk, :], qt, preferred_element_type=F32)
        s_s[h, j * tk:(j + 1) * tk, :] = s
        mm = s[0:8, :]
        for r in range(1, tk // 8):
            mm = jnp.maximum(mm, s[r * 8:(r + 1) * 8, :])
        return mm

    def set_query_max(mm, h):
        m_s[h] = jnp.broadcast_to(jnp.max(mm, axis=0, keepdims=True), (8, tq))

    def stage(n_kv):
        for h in heads:
            qt = qt_ref[0, h]
            mm = scores(qt, h, 0)
            for j in range(1, n_kv):
                mm = jnp.maximum(mm, scores(qt, h, j))
            set_query_max(mm, h)

    def attend(n_cur, n_next):
        m = [m_s[h, 0:1, :] for h in heads]
        qtn = [qtn_ref[0, h] for h in heads]
        acc = [jnp.zeros((MLA_VT_PAD, tq), F32) for _ in heads]
        mm = [None for _ in heads]
        for j in range(max(n_cur, n_next)):
            for h in heads:
                if j < n_cur:
                    p = jnp.exp2(s_s[h, j * tk:(j + 1) * tk, :] - m[h]).astype(MXU_DT)
                    acc[h] = acc[h] + jnp.dot(vt_ref[0, h, :, j * tk:(j + 1) * tk], p,
                                              preferred_element_type=F32)
                if j < n_next:
                    c = scores(qtn[h], h, j)
                    mm[h] = c if mm[h] is None else jnp.maximum(mm[h], c)
        for h in heads:
            set_query_max(mm[h], h)
            out_t = acc[h][0:MLA_V, :] / acc[h][MLA_V:MLA_V + 1, :]
            cs = slice(h * MLA_V, (h + 1) * MLA_V)
            o_ref[0, :, cs] = (jnp.transpose(out_t) * _silu(g_ref[0, :, cs].astype(F32))).astype(o_ref.dtype)

    pl.when(i == 0)(lambda: stage(n_first))
    if n_first == n_all_tiles:
        attend(n_all_tiles, n_all_tiles)
    else:
        pl.when(i == 0)(lambda: attend(n_first, n_all_tiles))
        pl.when(i > 0)(lambda: attend(n_all_tiles, n_all_tiles))


def _mla_attention(qt, kc, vt, p3, n_ctx_rows, with_ctx):
    B, H, S, _ = kc.shape
    tq, HP = ATT_TQ, ATT_HP
    assert n_ctx_rows == tq and ATT_TK == tq and H % HP == 0
    n_all_tiles = S // ATT_TK
    q_off = 0 if with_ctx else 1
    n_q = S // tq - q_off
    n_first = n_ctx_rows // ATT_TK if with_ctx else n_all_tiles
    gc0 = COL_GC // (HP * MLA_V)
    return pl.pallas_call(
        functools.partial(_attn_kernel, n_first=n_first, n_all_tiles=n_all_tiles),
        grid=(B, H // HP, n_q),
        in_specs=[pl.BlockSpec((1, HP, MLA_QK_PAD, tq), lambda b, h, i: (b, h, 0, i + q_off)),
                  pl.BlockSpec((1, HP, MLA_QK_PAD, tq),
                               lambda b, h, i: (b, h, 0, jnp.minimum(i + 1, n_q - 1) + q_off)),
                  pl.BlockSpec((1, HP, S, MLA_QK_PAD), lambda b, h, i: (b, h, 0, 0)),
                  pl.BlockSpec((1, HP, MLA_VT_PAD, S), lambda b, h, i: (b, h, 0, 0)),
                  pl.BlockSpec((1, tq, HP * MLA_V), lambda b, h, i: (b, i + q_off, gc0 + h))],
        out_specs=pl.BlockSpec((1, tq, HP * MLA_V), lambda b, h, i: (b, i, h)),
        out_shape=jax.ShapeDtypeStruct((B, n_q * tq, H * MLA_V), MXU_DT),
        scratch_shapes=[pltpu.VMEM((HP, S, tq), F32), pltpu.VMEM((HP, 8, tq), F32)],
        compiler_params=_params(("arbitrary", "arbitrary", "arbitrary")),
        name="mla_attention",
    )(qt, qt, kc, vt, p3)


def _merge_kernel(ya_ref, yb_ref, yc_ref, ma_ref, mb_ref, mc_ref, *refs, n_ctx_sub):
    if len(refs) == 8:
        ctx_ref, xl_ref, mod_ref, wa_ref, wb_ref, wc_ref, wo_ref, o_ref = refs
        x = jnp.where(pl.program_id(1) < n_ctx_sub, ctx_ref[0], xl_ref[0])
    else:
        x_ref, mod_ref, wa_ref, wb_ref, wc_ref, wo_ref, o_ref = refs
        x = x_ref[0]

    def down(y_ref, m_ref, w_ref):
        return jax.nn.sigmoid(m_ref[0].astype(F32)) * jnp.dot(y_ref[0], w_ref[...], preferred_element_type=F32)

    z = down(ya_ref, ma_ref, wa_ref) + down(yb_ref, mb_ref, wb_ref) + down(yc_ref, mc_ref, wc_ref)
    out = jnp.dot(z.astype(MXU_DT), wo_ref[...], preferred_element_type=F32)
    o_ref[0] = x + mod_ref[0, 2:3] * out


def _merge(ya, yb, yc, p3, srcs, modrows, wa, wb, wc, wo, row_off):
    B, S, _ = ya.shape
    tm = SUBTILE
    o = row_off // tm
    n_sub = S // tm
    n_ctx_sub = srcs[0].shape[1] // tm
    act = pl.BlockSpec((1, tm, D_MODEL), lambda b, i: (b, i + o, 0))
    o_c = o - (S - yc.shape[1]) // tm
    act_c = pl.BlockSpec((1, tm, D_MODEL), lambda b, i: (b, i + o_c, 0))
    gm = lambda k: pl.BlockSpec((1, tm, D_MODEL), lambda b, i: (b, i + o, COL_GM // D_MODEL + k))
    wspec = pl.BlockSpec((D_MODEL, D_MODEL), lambda b, i: (0, 0))
    if len(srcs) == 2:
        assert row_off == 0
        src_specs = [pl.BlockSpec((1, tm, D_MODEL), lambda b, i: (b, jnp.minimum(i, n_ctx_sub - 1), 0)),
                     pl.BlockSpec((1, tm, D_MODEL), lambda b, i: (b, jnp.maximum(i - n_ctx_sub, 0), 0))]
    else:
        src_specs = [act]
    return pl.pallas_call(
        functools.partial(_merge_kernel, n_ctx_sub=n_ctx_sub),
        grid=(B, n_sub - o),
        in_specs=[act, act, act_c, gm(0), gm(1), gm(2)] + src_specs
                 + [pl.BlockSpec((1, 8, D_MODEL), lambda b, i: (b * n_sub + i + o, 0, 0)),
                    wspec, wspec, wspec, wspec],
        out_specs=pl.BlockSpec((1, tm, D_MODEL), lambda b, i: (b, i, 0)),
        out_shape=jax.ShapeDtypeStruct((B, S - row_off, D_MODEL), F32),
        compiler_params=_params(("arbitrary", "arbitrary")),
        name="merge",
    )(ya, yb, yc, p3, p3, p3, *srcs, modrows, wa, wb, wc, wo)


def _rope_tables(n_ctx_rows, n_lat_rows, dim):
    rows = n_lat_rows // GRID_W
    row = np.repeat(np.arange(rows), GRID_W).astype(np.float64)
    col = np.tile(np.arange(GRID_W), rows).astype(np.float64)
    n_freq = dim // 4
    inv = ROPE_BASE ** (-np.arange(n_freq, dtype=np.float64) / n_freq)
    ang_r = row[:, None] * inv[None, :]
    ang_c = col[:, None] * inv[None, :]
    ang = np.concatenate([ang_r, ang_r, ang_c, ang_c], axis=-1)
    sign = np.tile(np.concatenate([-np.ones(n_freq), np.ones(n_freq)]), 2)
    cos = np.concatenate([np.ones((n_ctx_rows, dim)), np.cos(ang)], axis=0)
    sin = np.concatenate([np.zeros((n_ctx_rows, dim)), np.sin(ang) * sign], axis=0)
    return cos.astype(np.float32), sin.astype(np.float32)


def _permute_w_in(w):
    xa_to_gb = w[..., :6144]
    tail = w[..., 6144:6848]
    gc_gm = w[..., 6848:]
    pad = jnp.zeros(w.shape[:-1] + (P_COLS - w.shape[-1],), w.dtype)
    return jnp.concatenate([xa_to_gb, gc_gm, tail, pad], axis=-1).astype(MXU_DT)


def _split_heads(w, first):
    K = w.shape[0]
    w3 = w.reshape(K, MLA_HEADS, -1)
    return jnp.concatenate([w3[:, :, :first].reshape(K, -1), w3[:, :, first:].reshape(K, -1)],
                           axis=1).astype(MXU_DT)


def _pad_lanes(g):
    return jnp.concatenate([g, jnp.zeros((128 - g.shape[0],), g.dtype)]).reshape(1, 128)


def kernel(x, c, ctx, c_ctx, w_mod, b_mod, norm_gain, w_in, conv_w, conv_b, lru_wa, lru_ba, lru_wx, lru_bx,
           lru_lambda, ret_theta, ret_gain, mla_q_norm, mla_w_q_up, mla_kv_norm, mla_w_kv_up,
           mla_g_qn, mla_g_qr, mla_g_kn, mla_g_kr, w_down_a, w_down_b, w_down_c, w_out):
    B, T, D = x.shape
    L = ctx.shape[1]
    S = L + T
    depth = w_in.shape[0]
    assert D == D_MODEL and L % SUBTILE == 0 and T % SUBTILE == 0 and T % GRID_W == 0

    cos_r, sin_r = _rope_tables(L, T, RET_HEAD_DIM)
    cos_m, sin_m = _rope_tables(L, T, MLA_ROPE)
    pad64 = lambda t: np.concatenate([t, t], axis=1)
    cos_m, sin_m = pad64(cos_m), pad64(sin_m)

    n_rows = ((B + 1 + 7) // 8) * 8
    cc = jnp.zeros((n_rows, D), F32).at[:B].set(c).at[B].set(c_ctx)
    n_sub = S // SUBTILE
    sub = jnp.arange(B * n_sub)
    mod_row = jnp.where(sub % n_sub < L // SUBTILE, B, sub // n_sub)

    srcs = (ctx, x) if depth > 1 else (jnp.concatenate([ctx, x], axis=1),)
    for l in range(depth):
        last = l == depth - 1
        mod = _modulation(cc, w_mod, b_mod, l)
        mod3 = mod.reshape(n_rows, 3, D)[mod_row]
        modrows = jnp.concatenate([mod3, jnp.zeros((B * n_sub, 5, D), F32)], axis=1)

        p3 = _inproj(srcs, modrows, norm_gain[l], _permute_w_in(w_in[l]))

        w_cat = jnp.concatenate([lru_wa[l], lru_wx[l]], axis=-1).astype(MXU_DT)
        ya = _lru_branch(p3, conv_w[l], conv_b[l], w_cat, lru_ba[l], lru_bx[l], lru_lambda[l], L)
        yb = _ret_branch(p3, cos_r, sin_r, ret_theta[l], ret_gain[l], L)
        qc, kc, vc = _mla_prep(p3, _split_heads(mla_w_q_up[l], MLA_NOPE), _split_heads(mla_w_kv_up[l], MLA_NOPE),
                               mla_q_norm[l], mla_kv_norm[l], mla_g_qn[l],
                               jnp.concatenate([mla_g_qr[l], mla_g_qr[l]]).reshape(1, 128),
                               mla_g_kn[l], _pad_lanes(mla_g_kr[l]), cos_m, sin_m)
        yc = _mla_attention(qc, kc, vc, p3, L, not last)
        xs = _merge(ya, yb, yc, p3, srcs, modrows,
                    w_down_a[l].astype(MXU_DT), w_down_b[l].astype(MXU_DT), w_down_c[l].astype(MXU_DT),
                    w_out[l].astype(MXU_DT), L if last else 0)
        srcs = (xs,)
    return xs
```
